```python
import jax, jax.numpy as jnp
from jax import lax
import numpy as np

D_MODEL = 2048
BATCH = 1
SEQ = 16384
DEPTH = 1

N_HEADS = 8
N_KV_HEADS = 2
HEAD_DIM = 128
ATTN_WIDTH = N_HEADS * HEAD_DIM
KV_WIDTH = N_KV_HEADS * HEAD_DIM
Q_BLOCK = 128
ROPE_THETA = 10000.0
ROPE_PAIRS = HEAD_DIM // 4
GRID_W = 64
LRU_WIDTH = D_MODEL // 2
LRU_BLOCKS = 8
LRU_BW = LRU_WIDTH // LRU_BLOCKS
LRU_C = 8.0
CONV_W = 4
CONV_PAD = (2, 1)
MIX_WIDTH = ATTN_WIDTH + LRU_WIDTH
IN_WIDTH = ATTN_WIDTH + 2 * KV_WIDTH + ATTN_WIDTH + 2 * LRU_WIDTH
SPLITS = (ATTN_WIDTH,
          ATTN_WIDTH + KV_WIDTH,
          ATTN_WIDTH + 2 * KV_WIDTH,
          2 * ATTN_WIDTH + 2 * KV_WIDTH,
          2 * ATTN_WIDTH + 2 * KV_WIDTH + LRU_WIDTH)
EPS = 1e-6

kernel_name = "hymba_griffin_axialrope_hybrid_block"


def rms_norm(x, w):
    xf = x.astype(jnp.float32)
    y = xf * lax.rsqrt(jnp.mean(xf * xf, axis=-1, keepdims=True) + EPS)
    return (y * w.astype(jnp.float32)).astype(x.dtype)


def axial_rope_tables(seq_len):
    rows = seq_len // GRID_W
    row = jnp.repeat(jnp.arange(rows, dtype=jnp.float32), GRID_W)
    col = jnp.tile(jnp.arange(GRID_W, dtype=jnp.float32), rows)
    inv_freq = ROPE_THETA ** (-jnp.arange(ROPE_PAIRS, dtype=jnp.float32) / ROPE_PAIRS)
    ang_r = row[:, None] * inv_freq[None, :]
    ang_c = col[:, None] * inv_freq[None, :]
    return jnp.cos(ang_r), jnp.sin(ang_r), jnp.cos(ang_c), jnp.sin(ang_c)


def rope_half(x, cos, sin):
    c = cos[None, :, None, :]
    s = sin[None, :, None, :]
    x1, x2 = jnp.split(x, 2, axis=-1)
    return jnp.concatenate([x1 * c - x2 * s, x2 * c + x1 * s], axis=-1)


def apply_axial_rope(x, tables):
    cr, sr, cc, sc = tables
    xf = x.astype(jnp.float32)
    x_row, x_col = jnp.split(xf, 2, axis=-1)
    out = jnp.concatenate([rope_half(x_row, cr, sr), rope_half(x_col, cc, sc)], axis=-1)
    return out.astype(x.dtype)


def attention_group(q, k, v, q_norm_w, k_norm_w):
    B, S, _ = q.shape
    G = N_HEADS // N_KV_HEADS
    tables = axial_rope_tables(S)
    q = rms_norm(q.reshape(B, S, N_HEADS, HEAD_DIM), q_norm_w)
    k = rms_norm(k.reshape(B, S, N_KV_HEADS, HEAD_DIM), k_norm_w)
    q = apply_axial_rope(q, tables)
    k = apply_axial_rope(k, tables)
    v = v.reshape(B, S, N_KV_HEADS, HEAD_DIM)
    scale = HEAD_DIM ** -0.5
    nb = S // Q_BLOCK
    q_blocks = q.reshape(B, nb, Q_BLOCK, N_KV_HEADS, G, HEAD_DIM).transpose(1, 0, 3, 4, 2, 5)
    k_t = k.transpose(0, 2, 1, 3)
    v_t = v.transpose(0, 2, 1, 3)

    def block(qb):
        s = jnp.einsum('bkgqd,bksd->bkgqs', qb.astype(jnp.float32), k_t.astype(jnp.float32)) * scale
        p = jax.nn.softmax(s, axis=-1)
        return jnp.einsum('bkgqs,bksd->bkgqd', p, v_t.astype(jnp.float32)).astype(q.dtype)

    out = lax.map(block, q_blocks)
    return out.transpose(1, 0, 4, 2, 3, 5).reshape(B, S, ATTN_WIDTH)


def block_diag(x, w):
    B, S, _ = x.shape
    xb = x.reshape(B, S, LRU_BLOCKS, LRU_BW)
    return jnp.einsum('bsnc,ncd->bsnd', xb, w).reshape(B, S, LRU_WIDTH)


def linear_scan(a, b, reverse):
    def op(c1, c2):
        a1, b1 = c1
        a2, b2 = c2
        return a1 * a2, a2 * b1 + b2
    _, h = lax.associative_scan(op, (a, b), axis=1, reverse=reverse)
    return h


def rg_lru_direction(xf, wa, ba, wx, bx, lam, reverse):
    r = jax.nn.sigmoid(block_diag(xf, wa) + ba)
    i = jax.nn.sigmoid(block_diag(xf, wx) + bx)
    log_a = -LRU_C * r * jax.nn.softplus(-lam)
    a = jnp.exp(log_a)
    mult = jnp.sqrt(-jnp.expm1(2.0 * log_a))
    return linear_scan(a, mult * (i * xf), reverse)


def lru_group(xr, conv_w, conv_b, lru_wa, lru_ba, lru_wx, lru_bx, lru_lambda):
    xc = lax.conv_general_dilated(xr, conv_w.astype(xr.dtype), window_strides=(1,),
                                  padding=[CONV_PAD],
                                  dimension_numbers=('NWC', 'WIO', 'NWC'),
                                  feature_group_count=LRU_WIDTH) + conv_b
    xf = xc.astype(jnp.float32)
    h_fwd = rg_lru_direction(xf, lru_wa[0].astype(jnp.float32), lru_ba[0].astype(jnp.float32),
                             lru_wx[0].astype(jnp.float32), lru_bx[0].astype(jnp.float32),
                             lru_lambda[0].astype(jnp.float32), reverse=False)
    h_bwd = rg_lru_direction(xf, lru_wa[1].astype(jnp.float32), lru_ba[1].astype(jnp.float32),
                             lru_wx[1].astype(jnp.float32), lru_bx[1].astype(jnp.float32),
                             lru_lambda[1].astype(jnp.float32), reverse=True)
    return (h_fwd + h_bwd).astype(xr.dtype)


def setup_inputs(seed: int = 0) -> dict:
    key = jax.random.key(seed)
    ks = jax.random.split(key, 16)
    f32 = jnp.float32
    x = jax.random.normal(ks[0], (BATCH, SEQ, D_MODEL), f32)
    norm_w = 1.0 + 0.02 * jax.random.normal(ks[1], (D_MODEL,), f32)
    w_in = jax.random.normal(ks[2], (D_MODEL, IN_WIDTH), f32) * D_MODEL ** -0.5
    q_norm_w = 1.0 + 0.02 * jax.random.normal(ks[3], (HEAD_DIM,), f32)
    k_norm_w = 1.0 + 0.02 * jax.random.normal(ks[4], (HEAD_DIM,), f32)
    conv_w = jax.random.normal(ks[5], (CONV_W, 1, LRU_WIDTH), f32) * CONV_W ** -0.5
    conv_b = 0.01 * jax.random.normal(ks[6], (LRU_WIDTH,), f32)
    lru_wa = jax.random.normal(ks[7], (2, LRU_BLOCKS, LRU_BW, LRU_BW), f32) * LRU_BW ** -0.5
    lru_ba = 0.01 * jax.random.normal(ks[8], (2, LRU_WIDTH), f32)
    lru_wx = jax.random.normal(ks[9], (2, LRU_BLOCKS, LRU_BW, LRU_BW), f32) * LRU_BW ** -0.5
    lru_bx = 0.01 * jax.random.normal(ks[10], (2, LRU_WIDTH), f32)
    u = jax.random.uniform(ks[11], (2, LRU_WIDTH), f32, minval=0.9, maxval=0.999)
    a0 = u ** (1.0 / LRU_C)
    lru_lambda = jnp.log(a0) - jnp.log1p(-a0)
    attn_norm_w = 1.0 + 0.02 * jax.random.normal(ks[12], (ATTN_WIDTH,), f32)
    lru_norm_w = 1.0 + 0.02 * jax.random.normal(ks[13], (LRU_WIDTH,), f32)
    w_out = jax.random.normal(ks[14], (MIX_WIDTH, D_MODEL), f32) * MIX_WIDTH ** -0.5
    return {"x": x, "norm_w": norm_w, "w_in": w_in, "q_norm_w": q_norm_w, "k_norm_w": k_norm_w,
            "conv_w": conv_w, "conv_b": conv_b, "lru_wa": lru_wa, "lru_ba": lru_ba,
            "lru_wx": lru_wx, "lru_bx": lru_bx, "lru_lambda": lru_lambda,
            "attn_norm_w": attn_norm_w, "lru_norm_w": lru_norm_w, "w_out": w_out}


def hybrid_layer(x, norm_w, w_in, q_norm_w, k_norm_w, conv_w, conv_b, lru_wa, lru_ba, lru_wx,
                 lru_bx, lru_lambda, attn_norm_w, lru_norm_w, w_out):
    h = rms_norm(x, norm_w)
    proj = jnp.einsum('bsd,de->bse', h, w_in)
    q, k, v, g_attn, xr, g_lru = jnp.split(proj, SPLITS, axis=-1)
    attn_out = attention_group(q, k, v, q_norm_w, k_norm_w)
    lru_out = lru_group(xr, conv_w, conv_b, lru_wa, lru_ba, lru_wx, lru_bx, lru_lambda)
    mixed = jnp.concatenate([rms_norm(attn_out, attn_norm_w) * jax.nn.silu(g_attn),
                             rms_norm(lru_out, lru_norm_w) * jax.nn.silu(g_lru)], axis=-1)
    return x + jnp.einsum('bse,ed->bsd', mixed, w_out)


def reference(x, norm_w, w_in, q_norm_w, k_norm_w, conv_w, conv_b, lru_wa, lru_ba, lru_wx,
              lru_bx, lru_lambda, attn_norm_w, lru_norm_w, w_out):
    for _ in range(DEPTH):
        x = hybrid_layer(x, norm_w, w_in, q_norm_w, k_norm_w, conv_w, conv_b, lru_wa, lru_ba,
                         lru_wx, lru_bx, lru_lambda, attn_norm_w, lru_norm_w, w_out)
    return x
```

```python
import functools

import jax
import jax.numpy as jnp
from jax import lax
from jax.experimental import pallas as pl
from jax.experimental.pallas import tpu as pltpu

F32 = jnp.float32
BF16 = jnp.bfloat16

D_MODEL = 2048
N_HEADS = 8
N_KV_HEADS = 2
GROUP = N_HEADS // N_KV_HEADS
HEAD_DIM = 128
ATTN_WIDTH = N_HEADS * HEAD_DIM
KV_WIDTH = N_KV_HEADS * HEAD_DIM
ROPE_THETA = 10000.0
ROPE_PAIRS = HEAD_DIM // 4
GRID_W = 64
LRU_WIDTH = D_MODEL // 2
LRU_BLOCKS = 8
LRU_BW = LRU_WIDTH // LRU_BLOCKS
LRU_C = 8.0
CONV_W = 4
CONV_LEFT = 2
MIX_WIDTH = ATTN_WIDTH + LRU_WIDTH
IN_WIDTH = 2 * ATTN_WIDTH + 2 * KV_WIDTH + 2 * LRU_WIDTH
EPS = 1e-6

COL_Q = 0
COL_K = ATTN_WIDTH
COL_V = ATTN_WIDTH + KV_WIDTH
COL_GA = ATTN_WIDTH + 2 * KV_WIDTH
COL_XR = COL_GA + ATTN_WIDTH
COL_GL = COL_XR + LRU_WIDTH

SUBLANES = 8
LANES = 128
VMEM_LIMIT_CAP = 60000 * 1024
MIB = 1024 * 1024

NEG_BIG = -1e30


def _vmem_limit(estimate_bytes):
    return int(min(VMEM_LIMIT_CAP, estimate_bytes + 8 * MIB))


def _sigmoid(x):
    return 1.0 / (1.0 + jnp.exp(-x))


PROJ_TM = 256
PROJ_TN = 512


def _in_proj_kernel(x_ref, nw_ref, w_ref, qw_ref, kw_ref, cos_ref, sin_ref,
                    q_ref, k_ref, v_ref, ga_ref, xr_ref, gl_ref):
    x = x_ref[...]
    ms = jnp.mean(x * x, axis=-1, keepdims=True)
    h = (x * lax.rsqrt(ms + EPS) * nw_ref[...]).astype(BF16)
    cos = cos_ref[...]
    sin = sin_ref[...]
    lane = lax.broadcasted_iota(jnp.int32, (1, HEAD_DIM), 1)
    low_half = (lane % (2 * ROPE_PAIRS)) < ROPE_PAIRS

    def proj(col, width):
        return jnp.dot(h, w_ref[:, col:col + width], preferred_element_type=F32)

    def norm_rope(seg, gain, scale):
        ms_h = jnp.mean(seg * seg, axis=-1, keepdims=True)
        y = seg * lax.rsqrt(ms_h + EPS) * gain
        partner = jnp.where(low_half, pltpu.roll(y, HEAD_DIM - ROPE_PAIRS, 1),
                            pltpu.roll(y, ROPE_PAIRS, 1))
        return (y * cos + partner * sin) * scale

    scale = HEAD_DIM ** -0.5
    qw = qw_ref[...]
    kw = kw_ref[...]
    for cb in range(ATTN_WIDTH // PROJ_TN):
        acc = proj(COL_Q + cb * PROJ_TN, PROJ_TN)
        for hh in range(PROJ_TN // HEAD_DIM):
            seg = acc[:, hh * HEAD_DIM:(hh + 1) * HEAD_DIM]
            q_ref[cb * (PROJ_TN // HEAD_DIM) + hh] = norm_rope(seg, qw, scale).astype(BF16)
    acc = proj(COL_K, 2 * KV_WIDTH)
    for hh in range(N_KV_HEADS):
        k_ref[hh] = norm_rope(acc[:, hh * HEAD_DIM:(hh + 1) * HEAD_DIM], kw, 1.0).astype(BF16)
        v_ref[hh] = acc[:, KV_WIDTH + hh * HEAD_DIM:KV_WIDTH + (hh + 1) * HEAD_DIM].astype(BF16)
    for cb in range(ATTN_WIDTH // PROJ_TN):
        g = proj(COL_GA + cb * PROJ_TN, PROJ_TN)
        ga_ref[:, cb * PROJ_TN:(cb + 1) * PROJ_TN] = g * _sigmoid(g)
    for cb in range(LRU_WIDTH // PROJ_TN):
        xr_ref[:, cb * PROJ_TN:(cb + 1) * PROJ_TN] = proj(COL_XR + cb * PROJ_TN, PROJ_TN)
    for cb in range(LRU_WIDTH // PROJ_TN):
        g = proj(COL_GL + cb * PROJ_TN, PROJ_TN)
        gl_ref[:, cb * PROJ_TN:(cb + 1) * PROJ_TN] = g * _sigmoid(g)


def _in_proj(x2, norm_w, w_in_bf16, q_norm_w, k_norm_w, cos_t, sin_t):
    S = x2.shape[0]
    tm = PROJ_TM
    const = lambda i: (0, 0)
    est = (2 * tm * D_MODEL * 4 + D_MODEL * IN_WIDTH * 2 + 4 * tm * HEAD_DIM * 4
           + 2 * (tm * ATTN_WIDTH * 2 + 2 * tm * KV_WIDTH * 2 + 3 * tm * ATTN_WIDTH * 4)
           + tm * D_MODEL * 6 + 4 * tm * PROJ_TN * 4)
    return pl.pallas_call(
        _in_proj_kernel,
        grid=(S // tm,),
        in_specs=[
            pl.BlockSpec((tm, D_MODEL), lambda i: (i, 0)),
            pl.BlockSpec((1, D_MODEL), const),
            pl.BlockSpec((D_MODEL, IN_WIDTH), const, pipeline_mode=pl.Buffered(1)),
            pl.BlockSpec((1, HEAD_DIM), const),
            pl.BlockSpec((1, HEAD_DIM), const),
            pl.BlockSpec((tm, HEAD_DIM), lambda i: (i, 0)),
            pl.BlockSpec((tm, HEAD_DIM), lambda i: (i, 0)),
        ],
        out_specs=[
            pl.BlockSpec((N_HEADS, tm, HEAD_DIM), lambda i: (0, i, 0)),
            pl.BlockSpec((N_KV_HEADS, tm, HEAD_DIM), lambda i: (0, i, 0)),
            pl.BlockSpec((N_KV_HEADS, tm, HEAD_DIM), lambda i: (0, i, 0)),
            pl.BlockSpec((tm, ATTN_WIDTH), lambda i: (i, 0)),
            pl.BlockSpec((tm, LRU_WIDTH), lambda i: (i, 0)),
            pl.BlockSpec((tm, LRU_WIDTH), lambda i: (i, 0)),
        ],
        out_shape=[
            jax.ShapeDtypeStruct((N_HEADS, S, HEAD_DIM), BF16),
            jax.ShapeDtypeStruct((N_KV_HEADS, S, HEAD_DIM), BF16),
            jax.ShapeDtypeStruct((N_KV_HEADS, S, HEAD_DIM), BF16),
            jax.ShapeDtypeStruct((S, ATTN_WIDTH), F32),
            jax.ShapeDtypeStruct((S, LRU_WIDTH), F32),
            jax.ShapeDtypeStruct((S, LRU_WIDTH), F32),
        ],
        compiler_params=pltpu.CompilerParams(
            dimension_semantics=("arbitrary",), vmem_limit_bytes=_vmem_limit(est)),
        name="in_proj",
    )(x2, norm_w, w_in_bf16, q_norm_w, k_norm_w, cos_t, sin_t)


ATTN_TQ = 128
ATTN_TK = 512


def _attn_kernel(q_ref, k_ref, v_ref, o_ref, m_ref, l_ref, acc_ref, *, tq, tk):
    rows = GROUP * tq
    q = q_ref[...].reshape(rows, HEAD_DIM)
    m_ref[...] = jnp.full((rows, 1), NEG_BIG, F32)
    l_ref[...] = jnp.zeros((rows, 1), F32)
    acc_ref[...] = jnp.zeros((rows, HEAD_DIM), F32)
    n_kv = k_ref.shape[0] // tk

    def body(j, _):
        start = pl.multiple_of(j * tk, tk)
        kt = k_ref[pl.ds(start, tk), :]
        vt = v_ref[pl.ds(start, tk), :]
        s = lax.dot_general(q, kt, (((1,), (1,)), ((), ())), preferred_element_type=F32)
        m_old = m_ref[...]
        m_new = jnp.maximum(m_old, jnp.max(s, axis=-1, keepdims=True))
        p = jnp.exp(s - m_new)
        alpha = jnp.exp(m_old - m_new)
        l_ref[...] = alpha * l_ref[...] + jnp.sum(p, axis=-1, keepdims=True)
        acc_ref[...] = alpha * acc_ref[...] + jnp.dot(p.astype(BF16), vt,
                                                      preferred_element_type=F32)
        m_ref[...] = m_new
        return 0

    lax.fori_loop(0, n_kv, body, 0)
    out = acc_ref[...] / l_ref[...]
    for g in range(GROUP):
        o_ref[:, g * HEAD_DIM:(g + 1) * HEAD_DIM] = out[g * tq:(g + 1) * tq]


def _attention(q, k, v):
    S = q.shape[1]
    tq, tk = ATTN_TQ, ATTN_TK
    rows = GROUP * tq
    est = (2 * rows * HEAD_DIM * 2 + 2 * 2 * S * HEAD_DIM * 2 + 2 * tq * GROUP * HEAD_DIM * 4
           + 3 * rows * LANES * 4 + 4 * rows * tk * 4)
    return pl.pallas_call(
        functools.partial(_attn_kernel, tq=tq, tk=tk),
        grid=(N_KV_HEADS, S // tq),
        in_specs=[
            pl.BlockSpec((GROUP, tq, HEAD_DIM), lambda g, i: (g, i, 0)),
            pl.BlockSpec((None, S, HEAD_DIM), lambda g, i: (g, 0, 0)),
            pl.BlockSpec((None, S, HEAD_DIM), lambda g, i: (g, 0, 0)),
        ],
        out_specs=pl.BlockSpec((tq, GROUP * HEAD_DIM), lambda g, i: (i, g)),
        out_shape=jax.ShapeDtypeStruct((S, ATTN_WIDTH), F32),
        scratch_shapes=[
            pltpu.VMEM((rows, 1), F32),
            pltpu.VMEM((rows, 1), F32),
            pltpu.VMEM((rows, HEAD_DIM), F32),
        ],
        compiler_params=pltpu.CompilerParams(
            dimension_semantics=("arbitrary", "arbitrary"), vmem_limit_bytes=_vmem_limit(est)),
        name="attention",
    )(q, k, v)


LRU_T = 256
SCAN_SHIFTS = (1, 2, 4)


def _lru_gates(prev_ref, cur_ref, next_ref, ext_ref, a_ref, b_ref, has_prev, has_next,
               cw_ref, cb_ref, w_ref, bias_ref, lam_ref):
    T = cur_ref.shape[0]
    ext_ref[0:SUBLANES, :] = jnp.where(has_prev, prev_ref[...], 0.0)
    ext_ref[SUBLANES:SUBLANES + T, :] = cur_ref[...]
    ext_ref[SUBLANES + T:2 * SUBLANES + T, :] = jnp.where(has_next, next_ref[...], 0.0)
    base = SUBLANES - CONV_LEFT
    xc = cb_ref[...] + cw_ref[0:1, :] * ext_ref[base:base + T, :]
    for j in range(1, CONV_W):
        xc = xc + cw_ref[j:j + 1, :] * ext_ref[base + j:base + j + T, :]
    lam = lam_ref[...]
    neg = -lam
    softplus = jnp.maximum(neg, 0.0) + jnp.log1p(jnp.exp(-jnp.abs(neg)))
    decay = -LRU_C * softplus
    for n in range(LRU_BLOCKS):
        cols = slice(n * LRU_BW, (n + 1) * LRU_BW)
        xb = xc[:, cols]
        z = jnp.dot(xb.astype(BF16), w_ref[n], preferred_element_type=F32)
        r = _sigmoid(z[:, :LRU_BW] + bias_ref[0:1, cols])
        gate_i = _sigmoid(z[:, LRU_BW:] + bias_ref[1:2, cols])
        log_a = r * decay[:, cols]
        a = jnp.exp(log_a)
        mult = jnp.sqrt(1.0 - jnp.exp(2.0 * log_a))
        a_ref[:, cols] = a
        b_ref[:, cols] = mult * (gate_i * xb)


def _lru_scan(a_ref, b_ref, out_ref, carry_ref, reverse):
    T = a_ref.shape[0]
    n_groups = T // SUBLANES
    row = lax.broadcasted_iota(jnp.int32, (SUBLANES, LRU_BW), 0)

    def body(g, carry):
        gi = (n_groups - 1 - g) if reverse else g
        r0 = pl.multiple_of(gi * SUBLANES, SUBLANES)
        new = []
        for n in range(LRU_BLOCKS):
            cols = slice(n * LRU_BW, (n + 1) * LRU_BW)
            a = a_ref[pl.ds(r0, SUBLANES), cols]
            b = b_ref[pl.ds(r0, SUBLANES), cols]
            for s in SCAN_SHIFTS:
                if reverse:
                    valid = row < SUBLANES - s
                    shift = SUBLANES - s
                else:
                    valid = row >= s
                    shift = s
                b = jnp.where(valid, a * pltpu.roll(b, shift, 0) + b, b)
                a = jnp.where(valid, a * pltpu.roll(a, shift, 0), a)
            h = a * carry[n] + b
            out_ref[pl.ds(r0, SUBLANES), cols] = h
            edge = h[0:1, :] if reverse else h[SUBLANES - 1:SUBLANES, :]
            new.append(jnp.broadcast_to(edge, (SUBLANES, LRU_BW)))
        return tuple(new)

    init = tuple(carry_ref[:, n * LRU_BW:(n + 1) * LRU_BW] for n in range(LRU_BLOCKS))
    final = lax.fori_loop(0, n_groups, body, init)
    for n in range(LRU_BLOCKS):
        carry_ref[:, n * LRU_BW:(n + 1) * LRU_BW] = final[n]


def _lru_kernel(fp_ref, fc_ref, fn_ref, bp_ref, bc_ref, bn_ref, cw_ref, cb_ref, w_ref, bias_ref,
                lam_ref, hf_ref, hb_ref, ext_ref, a_ref, b_ref, carry_ref):
    c = pl.program_id(0)
    n_chunks = pl.num_programs(0)

    @pl.when(c == 0)
    def _():
        carry_ref[...] = jnp.zeros(carry_ref.shape, F32)

    _lru_gates(fp_ref, fc_ref, fn_ref, ext_ref, a_ref, b_ref, c > 0, c < n_chunks - 1,
               cw_ref, cb_ref, w_ref.at[0], bias_ref.at[0], lam_ref.at[0])
    _lru_scan(a_ref, b_ref, hf_ref, carry_ref.at[0], reverse=False)
    _lru_gates(bp_ref, bc_ref, bn_ref, ext_ref, a_ref, b_ref, c < n_chunks - 1, c > 0,
               cw_ref, cb_ref, w_ref.at[1], bias_ref.at[1], lam_ref.at[1])
    _lru_scan(a_ref, b_ref, hb_ref, carry_ref.at[1], reverse=True)


def _lru(xr, conv_w2, conv_b2, w_gate, bias_gate, lam):
    S = xr.shape[0]
    T = LRU_T
    n_chunks = S // T
    per = T // SUBLANES
    last_blk = S // SUBLANES - 1
    W = LRU_WIDTH
    halo = (SUBLANES, W)
    const2 = lambda c: (0, 0)
    est = (2 * 2 * (T * W * 4 + 2 * SUBLANES * W * 4) + 2 * 2 * T * W * 4
           + 2 * 2 * LRU_BLOCKS * LRU_BW * 2 * LRU_BW * 2 + (3 * T + 64) * W * 4 + 6 * T * W * 4)
    return pl.pallas_call(
        _lru_kernel,
        grid=(n_chunks,),
        in_specs=[
            pl.BlockSpec(halo, lambda c: (jnp.maximum(c * per - 1, 0), 0)),
            pl.BlockSpec((T, W), lambda c: (c, 0)),
            pl.BlockSpec(halo, lambda c: (jnp.minimum((c + 1) * per, last_blk), 0)),
            pl.BlockSpec(halo, lambda c: (jnp.maximum((n_chunks - 1 - c) * per - 1, 0), 0)),
            pl.BlockSpec((T, W), lambda c: (n_chunks - 1 - c, 0)),
            pl.BlockSpec(halo, lambda c: (jnp.minimum((n_chunks - c) * per, last_blk), 0)),
            pl.BlockSpec((CONV_W, W), const2),
            pl.BlockSpec((1, W), const2),
            pl.BlockSpec((2, LRU_BLOCKS, LRU_BW, 2 * LRU_BW), lambda c: (0, 0, 0, 0)),
            pl.BlockSpec((2, 2, W), lambda c: (0, 0, 0)),
            pl.BlockSpec((2, 1, W), lambda c: (0, 0, 0)),
        ],
        out_specs=[
            pl.BlockSpec((T, W), lambda c: (c, 0)),
            pl.BlockSpec((T, W), lambda c: (n_chunks - 1 - c, 0)),
        ],
        out_shape=[jax.ShapeDtypeStruct((S, W), F32), jax.ShapeDtypeStruct((S, W), F32)],
        scratch_shapes=[
            pltpu.VMEM((T + 2 * SUBLANES, W), F32),
            pltpu.VMEM((T, W), F32),
            pltpu.VMEM((T, W), F32),
            pltpu.VMEM((2, SUBLANES, W), F32),
        ],
        compiler_params=pltpu.CompilerParams(
            dimension_semantics=("arbitrary",), vmem_limit_bytes=_vmem_limit(est)),
        name="rg_lru",
    )(xr, xr, xr, xr, xr, xr, conv_w2, conv_b2, w_gate, bias_gate, lam)


OUT_TM = 256


def _out_proj_kernel(ao_ref, ga_ref, hf_ref, hb_ref, gl_ref, x_ref, anw_ref, lnw_ref, w_ref,
                     o_ref):
    def normed(v, gain):
        ms = jnp.mean(v * v, axis=-1, keepdims=True)
        return v * lax.rsqrt(ms + EPS) * gain

    ya = (normed(ao_ref[...], anw_ref[...]) * ga_ref[...]).astype(BF16)
    yl = (normed(hf_ref[...] + hb_ref[...], lnw_ref[...]) * gl_ref[...]).astype(BF16)
    acc = jnp.dot(ya, w_ref[0:ATTN_WIDTH, :], preferred_element_type=F32)
    acc = acc + jnp.dot(yl, w_ref[ATTN_WIDTH:MIX_WIDTH, :], preferred_element_type=F32)
    o_ref[...] = x_ref[...] + acc


def _out_proj(attn_out, g_attn, h_fwd, h_bwd, g_lru, x2, attn_norm_w, lru_norm_w, w_out_bf16):
    S = x2.shape[0]
    tm = OUT_TM
    row = lambda i: (i, 0)
    const = lambda i: (0, 0)
    est = (2 * 5 * tm * ATTN_WIDTH * 4 + 2 * 2 * tm * D_MODEL * 4 + 2 * MIX_WIDTH * D_MODEL * 2
           + 4 * tm * D_MODEL * 4)
    return pl.pallas_call(
        _out_proj_kernel,
        grid=(S // tm,),
        in_specs=[
            pl.BlockSpec((tm, ATTN_WIDTH), row),
            pl.BlockSpec((tm, ATTN_WIDTH), row),
            pl.BlockSpec((tm, LRU_WIDTH), row),
            pl.BlockSpec((tm, LRU_WIDTH), row),
            pl.BlockSpec((tm, LRU_WIDTH), row),
            pl.BlockSpec((tm, D_MODEL), row),
            pl.BlockSpec((1, ATTN_WIDTH), const),
            pl.BlockSpec((1, LRU_WIDTH), const),
            pl.BlockSpec((MIX_WIDTH, D_MODEL), const),
        ],
        out_specs=pl.BlockSpec((tm, D_MODEL), row),
        out_shape=jax.ShapeDtypeStruct((S, D_MODEL), F32),
        compiler_params=pltpu.CompilerParams(
            dimension_semantics=("arbitrary",), vmem_limit_bytes=_vmem_limit(est)),
        name="out_proj",
    )(attn_out, g_attn, h_fwd, h_bwd, g_lru, x2, attn_norm_w, lru_norm_w, w_out_bf16)


def _rope_tables(seq_len):
    rows = seq_len // GRID_W
    row = jnp.repeat(jnp.arange(rows, dtype=F32), GRID_W)
    col = jnp.tile(jnp.arange(GRID_W, dtype=F32), rows)
    inv_freq = ROPE_THETA ** (-jnp.arange(ROPE_PAIRS, dtype=F32) / ROPE_PAIRS)
    ang_r = row[:, None] * inv_freq[None, :]
    ang_c = col[:, None] * inv_freq[None, :]
    cr, sr, cc, sc = jnp.cos(ang_r), jnp.sin(ang_r), jnp.cos(ang_c), jnp.sin(ang_c)
    return (jnp.concatenate([cr, cr, cc, cc], axis=-1),
            jnp.concatenate([-sr, sr, -sc, sc], axis=-1))


def _layer(x2, norm_w, w_in, q_norm_w, k_norm_w, conv_w, conv_b, lru_wa, lru_ba, lru_wx, lru_bx,
           lru_lambda, attn_norm_w, lru_norm_w, w_out):
    S = x2.shape[0]
    cos_t, sin_t = _rope_tables(S)
    q, k, v, g_attn, xr, g_lru = _in_proj(
        x2, norm_w.reshape(1, D_MODEL), w_in.astype(BF16), q_norm_w.reshape(1, HEAD_DIM),
        k_norm_w.reshape(1, HEAD_DIM), cos_t, sin_t)
    attn_out = _attention(q, k, v)
    w_gate = jnp.concatenate([lru_wa, lru_wx], axis=-1).astype(BF16)
    bias_gate = jnp.stack([lru_ba, lru_bx], axis=1)
    h_fwd, h_bwd = _lru(xr, conv_w.reshape(CONV_W, LRU_WIDTH), conv_b.reshape(1, LRU_WIDTH),
                        w_gate, bias_gate, lru_lambda.reshape(2, 1, LRU_WIDTH))
    return _out_proj(attn_out, g_attn, h_fwd, h_bwd, g_lru, x2,
                     attn_norm_w.reshape(1, ATTN_WIDTH), lru_norm_w.reshape(1, LRU_WIDTH),
                     w_out.astype(BF16))


def kernel(x, norm_w, w_in, q_norm_w, k_norm_w, conv_w, conv_b, lru_wa, lru_ba, lru_wx, lru_bx,
           lru_lambda, attn_norm_w, lru_norm_w, w_out):
    B, S, D = x.shape
    assert D == D_MODEL and S % GRID_W == 0
    outs = [_layer(x[b], norm_w, w_in, q_norm_w, k_norm_w, conv_w, conv_b, lru_wa, lru_ba,
                   lru_wx, lru_bx, lru_lambda, attn_norm_w, lru_norm_w, w_out)
            for b in range(B)]
    return jnp.stack(outs, axis=0) if B > 1 else outs[0][None]
```

```python
import functools
import math

import jax
import jax.numpy as jnp
from jax import lax
from jax.experimental import pallas as pl
from jax.experimental.pallas import tpu as pltpu

F32 = jnp.float32
BF16 = jnp.bfloat16

D_MODEL = 2048
N_HEADS = 8
N_KV_HEADS = 2
GROUP = N_HEADS // N_KV_HEADS
HEAD_DIM = 128
ATTN_WIDTH = N_HEADS * HEAD_DIM
KV_WIDTH = N_KV_HEADS * HEAD_DIM
ROPE_THETA = 10000.0
ROPE_PAIRS = HEAD_DIM // 4
GRID_W = 64
LRU_WIDTH = D_MODEL // 2
LRU_BLOCKS = 8
LRU_BW = LRU_WIDTH // LRU_BLOCKS
LRU_C = 8.0
CONV_W = 4
CONV_LEFT = 2
MIX_WIDTH = ATTN_WIDTH + LRU_WIDTH
IN_WIDTH = 2 * ATTN_WIDTH + 2 * KV_WIDTH + 2 * LRU_WIDTH
EPS = 1e-6

COL_Q = 0
COL_K = ATTN_WIDTH
COL_V = ATTN_WIDTH + KV_WIDTH
COL_GA = ATTN_WIDTH + 2 * KV_WIDTH
COL_XR = COL_GA + ATTN_WIDTH
COL_GL = COL_XR + LRU_WIDTH

SUBLANES = 8
LANES = 128
VMEM_LIMIT_CAP = 60000 * 1024
MIB = 1024 * 1024

NEG_BIG = -1e30
LOG2_E = math.log2(math.e)


def _vmem_limit(estimate_bytes):
    return int(min(VMEM_LIMIT_CAP, estimate_bytes + 8 * MIB))


def _sigmoid(x):
    return 1.0 / (1.0 + jnp.exp(-x))


PROJ_TM = 256
PROJ_TN = 512
KN_ROWS = SUBLANES // N_KV_HEADS


def _in_proj_kernel(x_ref, nw_ref, w_ref, qw_ref, kw_ref, cos_ref, sin_ref,
                    q_ref, k_ref, v_ref, kn_ref, ga_ref, xr_ref, gl_ref):
    x = x_ref[...]
    ms = jnp.mean(x * x, axis=-1, keepdims=True)
    h = (x * lax.rsqrt(ms + EPS) * nw_ref[...]).astype(BF16)
    cos = cos_ref[...]
    sin = sin_ref[...]
    lane = lax.broadcasted_iota(jnp.int32, (1, HEAD_DIM), 1)
    low_half = (lane % (2 * ROPE_PAIRS)) < ROPE_PAIRS

    def proj(col, width):
        return jnp.dot(h, w_ref[:, col:col + width], preferred_element_type=F32)

    def norm_rope(seg, gain, scale):
        ms_h = jnp.mean(seg * seg, axis=-1, keepdims=True)
        y = seg * lax.rsqrt(ms_h + EPS) * gain
        partner = jnp.where(low_half, pltpu.roll(y, HEAD_DIM - ROPE_PAIRS, 1),
                            pltpu.roll(y, ROPE_PAIRS, 1))
        return (y * cos + partner * sin) * scale

    scale = HEAD_DIM ** -0.5 * LOG2_E
    qw = qw_ref[...]
    kw = kw_ref[...]
    tm = x_ref.shape[0]
    for cb in range(ATTN_WIDTH // PROJ_TN):
        acc = proj(COL_Q + cb * PROJ_TN, PROJ_TN)
        for hh in range(PROJ_TN // HEAD_DIM):
            head = cb * (PROJ_TN // HEAD_DIM) + hh
            seg = acc[:, hh * HEAD_DIM:(hh + 1) * HEAD_DIM]
            qt = norm_rope(seg, qw, scale).T.astype(BF16)
            lane0 = (head % GROUP) * tm
            q_ref[head // GROUP, 0, :, lane0:lane0 + tm] = qt
    acc = proj(COL_K, 2 * KV_WIDTH)

    @pl.when(pl.program_id(0) == 0)
    def _():
        kn_ref[...] = jnp.zeros(kn_ref.shape, F32)

    for hh in range(N_KV_HEADS):
        kb = norm_rope(acc[:, hh * HEAD_DIM:(hh + 1) * HEAD_DIM], kw, 1.0).astype(BF16)
        k_ref[hh] = kb
        kf = kb.astype(F32)
        n2 = jnp.max(jnp.sum(kf * kf, axis=-1, keepdims=True), axis=0, keepdims=True)
        rows = slice(hh * KN_ROWS, (hh + 1) * KN_ROWS)
        kn_ref[rows, :] = jnp.maximum(kn_ref[rows, :], n2)
        vseg = acc[:, KV_WIDTH + hh * HEAD_DIM:KV_WIDTH + (hh + 1) * HEAD_DIM]
        v_ref[hh] = vseg.T.astype(BF16)
    for cb in range(ATTN_WIDTH // PROJ_TN):
        g = proj(COL_GA + cb * PROJ_TN, PROJ_TN)
        ga_ref[:, cb * PROJ_TN:(cb + 1) * PROJ_TN] = g * _sigmoid(g)
    for cb in range(LRU_WIDTH // PROJ_TN):
        xr_ref[:, cb * PROJ_TN:(cb + 1) * PROJ_TN] = proj(COL_XR + cb * PROJ_TN, PROJ_TN)
    for cb in range(LRU_WIDTH // PROJ_TN):
        g = proj(COL_GL + cb * PROJ_TN, PROJ_TN)
        gl_ref[:, cb * PROJ_TN:(cb + 1) * PROJ_TN] = g * _sigmoid(g)


def _in_proj(x2, norm_w, w_in_bf16, q_norm_w, k_norm_w, cos_t, sin_t):
    S = x2.shape[0]
    tm = PROJ_TM
    const = lambda i: (0, 0)
    est = (2 * tm * D_MODEL * 4 + D_MODEL * IN_WIDTH * 2 + 4 * tm * HEAD_DIM * 4
           + 2 * (tm * ATTN_WIDTH * 2 + 2 * tm * KV_WIDTH * 2 + 3 * tm * ATTN_WIDTH * 4)
           + tm * D_MODEL * 6 + 4 * tm * PROJ_TN * 4)
    return pl.pallas_call(
        _in_proj_kernel,
        grid=(S // tm,),
        in_specs=[
            pl.BlockSpec((tm, D_MODEL), lambda i: (i, 0)),
            pl.BlockSpec((1, D_MODEL), const),
            pl.BlockSpec((D_MODEL, IN_WIDTH), const, pipeline_mode=pl.Buffered(1)),
            pl.BlockSpec((1, HEAD_DIM), const),
            pl.BlockSpec((1, HEAD_DIM), const),
            pl.BlockSpec((tm, HEAD_DIM), lambda i: (i, 0)),
            pl.BlockSpec((tm, HEAD_DIM), lambda i: (i, 0)),
        ],
        out_specs=[
            pl.BlockSpec((N_KV_HEADS, 1, HEAD_DIM, GROUP * tm), lambda i: (0, i, 0, 0)),
            pl.BlockSpec((N_KV_HEADS, tm, HEAD_DIM), lambda i: (0, i, 0)),
            pl.BlockSpec((N_KV_HEADS, HEAD_DIM, tm), lambda i: (0, 0, i)),
            pl.BlockSpec((N_KV_HEADS * KN_ROWS, LANES), const),
            pl.BlockSpec((tm, ATTN_WIDTH), lambda i: (i, 0)),
            pl.BlockSpec((tm, LRU_WIDTH), lambda i: (i, 0)),
            pl.BlockSpec((tm, LRU_WIDTH), lambda i: (i, 0)),
        ],
        out_shape=[
            jax.ShapeDtypeStruct((N_KV_HEADS, S // tm, HEAD_DIM, GROUP * tm), BF16),
            jax.ShapeDtypeStruct((N_KV_HEADS, S, HEAD_DIM), BF16),
            jax.ShapeDtypeStruct((N_KV_HEADS, HEAD_DIM, S), BF16),
            jax.ShapeDtypeStruct((N_KV_HEADS * KN_ROWS, LANES), F32),
            jax.ShapeDtypeStruct((S, ATTN_WIDTH), F32),
            jax.ShapeDtypeStruct((S, LRU_WIDTH), F32),
            jax.ShapeDtypeStruct((S, LRU_WIDTH), F32),
        ],
        compiler_params=pltpu.CompilerParams(
            dimension_semantics=("arbitrary",), vmem_limit_bytes=_vmem_limit(est)),
        name="in_proj",
    )(x2, norm_w, w_in_bf16, q_norm_w, k_norm_w, cos_t, sin_t)


ATTN_TQ = PROJ_TM
ATTN_TK = 512


SOFTMAX_ROWS = 32


def _sublane_allmax(x):
    for shift in (4, 2, 1):
        x = jnp.maximum(x, pltpu.roll(x, shift, 0))
    return x


SAFE_LOGIT_BOUND = 55.0
BOUND_SLACK = 1.0 + 2.0 ** -10
LANE_BLOCK = 256


def _attn_kernel(kn_ref, qt_ref, k_ref, vt_ref, o_ref, m_ref, l_ref, acc_ref, s_buf, tmax_buf,
                 p_buf, alpha_buf, *, tq, tk):
    lanes = GROUP * tq
    n_kv = k_ref.shape[0] // tk
    n_groups = tk // SUBLANES
    chunk = SOFTMAX_ROWS // SUBLANES
    acc_shape = (HEAD_DIM // SUBLANES, SUBLANES, lanes)

    qf = qt_ref[...].astype(F32)
    qn2 = jnp.sum(qf * qf, axis=0, keepdims=True)
    kn = kn_ref[...]
    kn2 = jnp.where(pl.program_id(0) == 0, kn[0:1, :], kn[KN_ROWS:KN_ROWS + 1, :])
    bound = jnp.sqrt(qn2 * jnp.tile(kn2, (1, lanes // LANES))) * BOUND_SLACK
    single_pass = jnp.max(bound) <= SAFE_LOGIT_BOUND

    def kv_rows(j):
        return pl.ds(pl.multiple_of(j * tk, tk), tk)

    def finish():
        l_tot = jnp.sum(l_ref[...], axis=0, keepdims=True)
        out_t = acc_ref[...].reshape(HEAD_DIM, lanes) / l_tot
        for h in range(GROUP):
            o_ref[:, h * HEAD_DIM:(h + 1) * HEAD_DIM] = out_t[:, h * tq:(h + 1) * tq].T

    l_ref[...] = jnp.zeros((SUBLANES, lanes), F32)
    acc_ref[...] = jnp.zeros(acc_shape, F32)

    @pl.when(single_pass)
    def _():
        m_ref[...] = jnp.broadcast_to(bound, (SUBLANES, lanes))

        def probs(j, slot):
            kt = k_ref[kv_rows(j), :]
            for lb in range(lanes // LANE_BLOCK):
                cols = slice(lb * LANE_BLOCK, (lb + 1) * LANE_BLOCK)
                s = jnp.dot(kt, qt_ref[:, cols], preferred_element_type=F32)
                p = jnp.exp2(s.reshape(n_groups, SUBLANES, LANE_BLOCK) - m_ref[:, cols])
                l_ref[:, cols] += jnp.sum(p, axis=0)
                p_buf[slot, :, cols] = p.reshape(tk, LANE_BLOCK).astype(BF16)

        def values(j, slot):
            pv = jnp.dot(vt_ref[:, kv_rows(j)], p_buf[slot], preferred_element_type=F32)
            acc_ref[...] += pv.reshape(acc_shape)

        probs(0, 0)

        def body(t, _):
            j = 2 * t
            probs(j + 1, 1)
            values(j, 0)
            probs(j + 2, 0)
            values(j + 1, 1)
            return 0

        lax.fori_loop(0, n_kv // 2 - 1, body, 0)
        probs(n_kv - 1, 1)
        values(n_kv - 2, 0)
        values(n_kv - 1, 1)
        finish()

    @pl.when(jnp.logical_not(single_pass))
    def _():
        def scores(j, slot):
            s = jnp.dot(k_ref[kv_rows(j), :], qt_ref[...], preferred_element_type=F32)
            s = s.reshape(n_groups, SUBLANES, lanes)
            s_buf[slot] = s
            tmax_buf[slot] = jnp.max(s, axis=0)

        def softmax(slot):
            m_old = m_ref[...]
            m_new = jnp.maximum(m_old, _sublane_allmax(tmax_buf[slot]))
            alpha = jnp.exp2(m_old - m_new)
            psum = jnp.zeros((SUBLANES, lanes), F32)
            for c in range(n_groups // chunk):
                s = s_buf[slot, c * chunk:(c + 1) * chunk]
                p = jnp.exp2(s - m_new)
                psum = psum + jnp.sum(p, axis=0)
                p_buf[slot, c * SOFTMAX_ROWS:(c + 1) * SOFTMAX_ROWS, :] = (
                    p.reshape(SOFTMAX_ROWS, lanes).astype(BF16))
            l_ref[...] = alpha * l_ref[...] + psum
            m_ref[...] = m_new
            alpha_buf[slot] = alpha

        def values(j, slot):
            pv = jnp.dot(vt_ref[:, kv_rows(j)], p_buf[slot], preferred_element_type=F32)
            acc_ref[...] = alpha_buf[slot] * acc_ref[...] + pv.reshape(acc_shape)

        m_ref[...] = jnp.full((SUBLANES, lanes), NEG_BIG, F32)
        p_buf[1] = jnp.zeros((tk, lanes), BF16)
        alpha_buf[1] = jnp.ones((SUBLANES, lanes), F32)
        scores(0, 0)

        def body(t, _):
            for slot in range(2):
                cur = 2 * t + slot
                nxt = jnp.where(cur + 1 < n_kv, cur + 1, 0)
                prv = jnp.maximum(cur - 1, 0)
                scores(nxt, 1 - slot)
                softmax(slot)
                values(prv, 1 - slot)
            return 0

        lax.fori_loop(0, n_kv // 2, body, 0)
        values(n_kv - 1, 1)
        finish()


def _attention(k_norm2, q_t, k, v_t):
    S = k.shape[1]
    tq, tk = ATTN_TQ, ATTN_TK
    assert S % (2 * tk) == 0 and S >= 4 * tk and tk % SOFTMAX_ROWS == 0
    lanes = GROUP * tq
    est = (2 * HEAD_DIM * lanes * 2 + 2 * 2 * S * HEAD_DIM * 2 + 2 * tq * GROUP * HEAD_DIM * 4
           + (HEAD_DIM + 6 * SUBLANES) * lanes * 4 + 2 * tk * lanes * 6 + 2 * tk * lanes * 4)
    return pl.pallas_call(
        functools.partial(_attn_kernel, tq=tq, tk=tk),
        grid=(N_KV_HEADS, S // tq),
        in_specs=[
            pl.BlockSpec((N_KV_HEADS * KN_ROWS, LANES), lambda g, i: (0, 0)),
            pl.BlockSpec((None, None, HEAD_DIM, lanes), lambda g, i: (g, i, 0, 0)),
            pl.BlockSpec((None, S, HEAD_DIM), lambda g, i: (g, 0, 0)),
            pl.BlockSpec((None, HEAD_DIM, S), lambda g, i: (g, 0, 0)),
        ],
        out_specs=pl.BlockSpec((tq, GROUP * HEAD_DIM), lambda g, i: (i, g)),
        out_shape=jax.ShapeDtypeStruct((S, ATTN_WIDTH), F32),
        scratch_shapes=[
            pltpu.VMEM((SUBLANES, lanes), F32),
            pltpu.VMEM((SUBLANES, lanes), F32),
            pltpu.VMEM((HEAD_DIM // SUBLANES, SUBLANES, lanes), F32),
            pltpu.VMEM((2, tk // SUBLANES, SUBLANES, lanes), F32),
            pltpu.VMEM((2, SUBLANES, lanes), F32),
            pltpu.VMEM((2, tk, lanes), BF16),
            pltpu.VMEM((2, SUBLANES, lanes), F32),
        ],
        compiler_params=pltpu.CompilerParams(
            dimension_semantics=("arbitrary", "arbitrary"), vmem_limit_bytes=_vmem_limit(est)),
        name="attention",
    )(k_norm2, q_t, k, v_t)


LRU_T = 256
SCAN_SHIFTS = (1, 2, 4)


def _lru_gates(prev_ref, cur_ref, next_ref, ext_ref, a_ref, b_ref, has_prev, has_next,
               cw_ref, cb_ref, w_ref, bias_ref, lam_ref):
    T = cur_ref.shape[0]
    ext_ref[0:SUBLANES, :] = jnp.where(has_prev, prev_ref[...], 0.0)
    ext_ref[SUBLANES:SUBLANES + T, :] = cur_ref[...]
    ext_ref[SUBLANES + T:2 * SUBLANES + T, :] = jnp.where(has_next, next_ref[...], 0.0)
    base = SUBLANES - CONV_LEFT
    xc = cb_ref[...] + cw_ref[0:1, :] * ext_ref[base:base + T, :]
    for j in range(1, CONV_W):
        xc = xc + cw_ref[j:j + 1, :] * ext_ref[base + j:base + j + T, :]
    lam = lam_ref[...]
    neg = -lam
    softplus = jnp.maximum(neg, 0.0) + jnp.log1p(jnp.exp(-jnp.abs(neg)))
    decay = -LRU_C * softplus
    for n in range(LRU_BLOCKS):
        cols = slice(n * LRU_BW, (n + 1) * LRU_BW)
        xb = xc[:, cols]
        z = jnp.dot(xb.astype(BF16), w_ref[n], preferred_element_type=F32)
        r = _sigmoid(z[:, :LRU_BW] + bias_ref[0:1, cols])
        gate_i = _sigmoid(z[:, LRU_BW:] + bias_ref[1:2, cols])
        log_a = r * decay[:, cols]
        a = jnp.exp(log_a)
        mult = jnp.sqrt(1.0 - jnp.exp(2.0 * log_a))
        a_ref[:, cols] = a
        b_ref[:, cols] = mult * (gate_i * xb)


def _lru_scan(a_ref, b_ref, out_ref, carry_ref, reverse):
    T = a_ref.shape[0]
    n_groups = T // SUBLANES
    row = lax.broadcasted_iota(jnp.int32, (SUBLANES, LRU_BW), 0)

    def body(g, carry):
        gi = (n_groups - 1 - g) if reverse else g
        r0 = pl.multiple_of(gi * SUBLANES, SUBLANES)
        new = []
        for n in range(LRU_BLOCKS):
            cols = slice(n * LRU_BW, (n + 1) * LRU_BW)
            a = a_ref[pl.ds(r0, SUBLANES), cols]
            b = b_ref[pl.ds(r0, SUBLANES), cols]
            for s in SCAN_SHIFTS:
                if reverse:
                    valid = row < SUBLANES - s
                    shift = SUBLANES - s
                else:
                    valid = row >= s
                    shift = s
                b = jnp.where(valid, a * pltpu.roll(b, shift, 0) + b, b)
                a = jnp.where(valid, a * pltpu.roll(a, shift, 0), a)
            h = a * carry[n] + b
            out_ref[pl.ds(r0, SUBLANES), cols] = h
            edge = h[0:1, :] if reverse else h[SUBLANES - 1:SUBLANES, :]
            new.append(jnp.broadcast_to(edge, (SUBLANES, LRU_BW)))
        return tuple(new)

    init = tuple(carry_ref[:, n * LRU_BW:(n + 1) * LRU_BW] for n in range(LRU_BLOCKS))
    final = lax.fori_loop(0, n_groups, body, init)
    for n in range(LRU_BLOCKS):
        carry_ref[:, n * LRU_BW:(n + 1) * LRU_BW] = final[n]


def _lru_kernel(fp_ref, fc_ref, fn_ref, bp_ref, bc_ref, bn_ref, cw_ref, cb_ref, w_ref, bias_ref,
                lam_ref, hf_ref, hb_ref, ext_ref, a_ref, b_ref, carry_ref):
    c = pl.program_id(0)
    n_chunks = pl.num_programs(0)

    @pl.when(c == 0)
    def _():
        carry_ref[...] = jnp.zeros(carry_ref.shape, F32)

    _lru_gates(fp_ref, fc_ref, fn_ref, ext_ref, a_ref, b_ref, c > 0, c < n_chunks - 1,
               cw_ref, cb_ref, w_ref.at[0], bias_ref.at[0], lam_ref.at[0])
    _lru_scan(a_ref, b_ref, hf_ref, carry_ref.at[0], reverse=False)
    _lru_gates(bp_ref, bc_ref, bn_ref, ext_ref, a_ref, b_ref, c < n_chunks - 1, c > 0,
               cw_ref, cb_ref, w_ref.at[1], bias_ref.at[1], lam_ref.at[1])
    _lru_scan(a_ref, b_ref, hb_ref, carry_ref.at[1], reverse=True)


def _lru(xr, conv_w2, conv_b2, w_gate, bias_gate, lam):
    S = xr.shape[0]
    T = LRU_T
    n_chunks = S // T
    per = T // SUBLANES
    last_blk = S // SUBLANES - 1
    W = LRU_WIDTH
    halo = (SUBLANES, W)
    const2 = lambda c: (0, 0)
    est = (2 * 2 * (T * W * 4 + 2 * SUBLANES * W * 4) + 2 * 2 * T * W * 4
           + 2 * 2 * LRU_BLOCKS * LRU_BW * 2 * LRU_BW * 2 + (3 * T + 64) * W * 4 + 6 * T * W * 4)
    return pl.pallas_call(
        _lru_kernel,
        grid=(n_chunks,),
        in_specs=[
            pl.BlockSpec(halo, lambda c: (jnp.maximum(c * per - 1, 0), 0)),
            pl.BlockSpec((T, W), lambda c: (c, 0)),
            pl.BlockSpec(halo, lambda c: (jnp.minimum((c + 1) * per, last_blk), 0)),
            pl.BlockSpec(halo, lambda c: (jnp.maximum((n_chunks - 1 - c) * per - 1, 0), 0)),
            pl.BlockSpec((T, W), lambda c: (n_chunks - 1 - c, 0)),
            pl.BlockSpec(halo, lambda c: (jnp.minimum((n_chunks - c) * per, last_blk), 0)),
            pl.BlockSpec((CONV_W, W), const2),
            pl.BlockSpec((1, W), const2),
            pl.BlockSpec((2, LRU_BLOCKS, LRU_BW, 2 * LRU_BW), lambda c: (0, 0, 0, 0)),
            pl.BlockSpec((2, 2, W), lambda c: (0, 0, 0)),
            pl.BlockSpec((2, 1, W), lambda c: (0, 0, 0)),
        ],
        out_specs=[
            pl.BlockSpec((T, W), lambda c: (c, 0)),
            pl.BlockSpec((T, W), lambda c: (n_chunks - 1 - c, 0)),
        ],
        out_shape=[jax.ShapeDtypeStruct((S, W), F32), jax.ShapeDtypeStruct((S, W), F32)],
        scratch_shapes=[
            pltpu.VMEM((T + 2 * SUBLANES, W), F32),
            pltpu.VMEM((T, W), F32),
            pltpu.VMEM((T, W), F32),
            pltpu.VMEM((2, SUBLANES, W), F32),
        ],
        compiler_params=pltpu.CompilerParams(
            dimension_semantics=("arbitrary",), vmem_limit_bytes=_vmem_limit(est)),
        name="rg_lru",
    )(xr, xr, xr, xr, xr, xr, conv_w2, conv_b2, w_gate, bias_gate, lam)


OUT_TM = 256


def _out_proj_kernel(ao_ref, ga_ref, hf_ref, hb_ref, gl_ref, x_ref, anw_ref, lnw_ref, w_ref,
                     o_ref):
    def normed(v, gain):
        ms = jnp.mean(v * v, axis=-1, keepdims=True)
        return v * lax.rsqrt(ms + EPS) * gain

    ya = (normed(ao_ref[...], anw_ref[...]) * ga_ref[...]).astype(BF16)
    yl = (normed(hf_ref[...] + hb_ref[...], lnw_ref[...]) * gl_ref[...]).astype(BF16)
    acc = jnp.dot(ya, w_ref[0:ATTN_WIDTH, :], preferred_element_type=F32)
    acc = acc + jnp.dot(yl, w_ref[ATTN_WIDTH:MIX_WIDTH, :], preferred_element_type=F32)
    o_ref[...] = x_ref[...] + acc


def _out_proj(attn_out, g_attn, h_fwd, h_bwd, g_lru, x2, attn_norm_w, lru_norm_w, w_out_bf16):
    S = x2.shape[0]
    tm = OUT_TM
    row = lambda i: (i, 0)
    const = lambda i: (0, 0)
    est = (2 * 5 * tm * ATTN_WIDTH * 4 + 2 * 2 * tm * D_MODEL * 4 + 2 * MIX_WIDTH * D_MODEL * 2
           + 4 * tm * D_MODEL * 4)
    return pl.pallas_call(
        _out_proj_kernel,
        grid=(S // tm,),
        in_specs=[
            pl.BlockSpec((tm, ATTN_WIDTH), row),
            pl.BlockSpec((tm, ATTN_WIDTH), row),
            pl.BlockSpec((tm, LRU_WIDTH), row),
            pl.BlockSpec((tm, LRU_WIDTH), row),
            pl.BlockSpec((tm, LRU_WIDTH), row),
            pl.BlockSpec((tm, D_MODEL), row),
            pl.BlockSpec((1, ATTN_WIDTH), const),
            pl.BlockSpec((1, LRU_WIDTH), const),
            pl.BlockSpec((MIX_WIDTH, D_MODEL), const),
        ],
        out_specs=pl.BlockSpec((tm, D_MODEL), row),
        out_shape=jax.ShapeDtypeStruct((S, D_MODEL), F32),
        compiler_params=pltpu.CompilerParams(
            dimension_semantics=("arbitrary",), vmem_limit_bytes=_vmem_limit(est)),
        name="out_proj",
    )(attn_out, g_attn, h_fwd, h_bwd, g_lru, x2, attn_norm_w, lru_norm_w, w_out_bf16)


def _rope_tables(seq_len):
    rows = seq_len // GRID_W
    row = jnp.repeat(jnp.arange(rows, dtype=F32), GRID_W)
    col = jnp.tile(jnp.arange(GRID_W, dtype=F32), rows)
    inv_freq = ROPE_THETA ** (-jnp.arange(ROPE_PAIRS, dtype=F32) / ROPE_PAIRS)
    ang_r = row[:, None] * inv_freq[None, :]
    ang_c = col[:, None] * inv_freq[None, :]
    cr, sr, cc, sc = jnp.cos(ang_r), jnp.sin(ang_r), jnp.cos(ang_c), jnp.sin(ang_c)
    return (jnp.concatenate([cr, cr, cc, cc], axis=-1),
            jnp.concatenate([-sr, sr, -sc, sc], axis=-1))


def _layer(x2, norm_w, w_in, q_norm_w, k_norm_w, conv_w, conv_b, lru_wa, lru_ba, lru_wx, lru_bx,
           lru_lambda, attn_norm_w, lru_norm_w, w_out):
    S = x2.shape[0]
    cos_t, sin_t = _rope_tables(S)
    q_t, k, v_t, k_norm2, g_attn, xr, g_lru = _in_proj(
        x2, norm_w.reshape(1, D_MODEL), w_in.astype(BF16), q_norm_w.reshape(1, HEAD_DIM),
        k_norm_w.reshape(1, HEAD_DIM), cos_t, sin_t)
    attn_out = _attention(k_norm2, q_t, k, v_t)
    w_gate = jnp.concatenate([lru_wa, lru_wx], axis=-1).astype(BF16)
    bias_gate = jnp.stack([lru_ba, lru_bx], axis=1)
    h_fwd, h_bwd = _lru(xr, conv_w.reshape(CONV_W, LRU_WIDTH), conv_b.reshape(1, LRU_WIDTH),
                        w_gate, bias_gate, lru_lambda.reshape(2, 1, LRU_WIDTH))
    return _out_proj(attn_out, g_attn, h_fwd, h_bwd, g_lru, x2,
                     attn_norm_w.reshape(1, ATTN_WIDTH), lru_norm_w.reshape(1, LRU_WIDTH),
                     w_out.astype(BF16))


def kernel(x, norm_w, w_in, q_norm_w, k_norm_w, conv_w, conv_b, lru_wa, lru_ba, lru_wx, lru_bx,
           lru_lambda, attn_norm_w, lru_norm_w, w_out):
    B, S, D = x.shape
    assert D == D_MODEL and S % GRID_W == 0
    outs = [_layer(x[b], norm_w, w_in, q_norm_w, k_norm_w, conv_w, conv_b, lru_wa, lru_ba,
                   lru_wx, lru_bx, lru_lambda, attn_norm_w, lru_norm_w, w_out)
            for b in range(B)]
    return jnp.stack(outs, axis=0) if B > 1 else outs[0][None]
```

```python
import functools
import math

import jax
import jax.numpy as jnp
from jax import lax
from jax.experimental import pallas as pl
from jax.experimental.pallas import tpu as pltpu

F32 = jnp.float32
BF16 = jnp.bfloat16

D_MODEL = 2048
N_HEADS = 8
N_KV_HEADS = 2
GROUP = N_HEADS // N_KV_HEADS
HEAD_DIM = 128
ATTN_WIDTH = N_HEADS * HEAD_DIM
KV_WIDTH = N_KV_HEADS * HEAD_DIM
ROPE_THETA = 10000.0
ROPE_PAIRS = HEAD_DIM // 4
GRID_W = 64
LRU_WIDTH = D_MODEL // 2
LRU_BLOCKS = 8
LRU_BW = LRU_WIDTH // LRU_BLOCKS
LRU_C = 8.0
CONV_W = 4
CONV_LEFT = 2
MIX_WIDTH = ATTN_WIDTH + LRU_WIDTH
IN_WIDTH = 2 * ATTN_WIDTH + 2 * KV_WIDTH + 2 * LRU_WIDTH
EPS = 1e-6

COL_Q = 0
COL_K = ATTN_WIDTH
COL_V = ATTN_WIDTH + KV_WIDTH
COL_GA = ATTN_WIDTH + 2 * KV_WIDTH
COL_XR = COL_GA + ATTN_WIDTH
COL_GL = COL_XR + LRU_WIDTH

SUBLANES = 8
LANES = 128
VMEM_LIMIT_CAP = 60000 * 1024
MIB = 1024 * 1024

NEG_BIG = -1e30
TINY = 1e-30
LOG2_E = math.log2(math.e)


def _vmem_limit(estimate_bytes):
    return int(min(VMEM_LIMIT_CAP, estimate_bytes + 8 * MIB))


def _sigmoid(x):
    return 0.5 * jnp.tanh(0.5 * x) + 0.5


PROJ_TM = 256
PROJ_TN = 512
KN_ROWS = SUBLANES // N_KV_HEADS


def _in_proj_kernel(x_ref, nw_ref, w_ref, qw_ref, kw_ref, rcos_ref, rsin_ref, ccos_ref, csin_ref,
                    q_ref, k_ref, v_ref, kn_ref, ga_ref, xr_ref, gl_ref):
    x = x_ref[...]
    ms = jnp.mean(x * x, axis=-1, keepdims=True)
    h = (x * lax.rsqrt(ms + EPS) * nw_ref[...]).astype(BF16)

    def rope_table(row_ref, col_ref):
        grid_rows = x_ref.shape[0] // GRID_W
        first = pl.program_id(0) * grid_rows
        lines = [jnp.broadcast_to(row_ref[pl.ds(first + r, 1), :], (GRID_W, HEAD_DIM))
                 for r in range(grid_rows)]
        return jnp.concatenate(lines, axis=0) + col_ref[...]

    cos = rope_table(rcos_ref, ccos_ref)
    sin = rope_table(rsin_ref, csin_ref)
    lane = lax.broadcasted_iota(jnp.int32, (1, HEAD_DIM), 1)
    low_half = (lane % (2 * ROPE_PAIRS)) < ROPE_PAIRS

    def proj(col, width):
        return jnp.dot(h, w_ref[:, col:col + width], preferred_element_type=F32)

    def norm_rope(seg, gain, scale):
        ms_h = jnp.mean(seg * seg, axis=-1, keepdims=True)
        y = seg * lax.rsqrt(ms_h + EPS) * gain
        partner = jnp.where(low_half, pltpu.roll(y, HEAD_DIM - ROPE_PAIRS, 1),
                            pltpu.roll(y, ROPE_PAIRS, 1))
        return (y * cos + partner * sin) * scale

    scale = HEAD_DIM ** -0.5 * LOG2_E
    qw = qw_ref[...]
    kw = kw_ref[...]
    tm = x_ref.shape[0]
    for cb in range(ATTN_WIDTH // PROJ_TN):
        acc = proj(COL_Q + cb * PROJ_TN, PROJ_TN)
        for hh in range(PROJ_TN // HEAD_DIM):
            head = cb * (PROJ_TN // HEAD_DIM) + hh
            seg = acc[:, hh * HEAD_DIM:(hh + 1) * HEAD_DIM]
            qt = norm_rope(seg, qw, scale).T.astype(BF16)
            lane0 = (head % GROUP) * tm
            q_ref[head // GROUP, 0, :, lane0:lane0 + tm] = qt
    acc = proj(COL_K, 2 * KV_WIDTH)

    @pl.when(pl.program_id(0) == 0)
    def _():
        kn_ref[...] = jnp.zeros(kn_ref.shape, F32)

    for hh in range(N_KV_HEADS):
        kb = norm_rope(acc[:, hh * HEAD_DIM:(hh + 1) * HEAD_DIM], kw, 1.0).astype(BF16)
        k_ref[hh] = kb
        kf = kb.astype(F32)
        n2 = jnp.max(jnp.sum(kf * kf, axis=-1, keepdims=True), axis=0, keepdims=True)
        rows = slice(hh * KN_ROWS, (hh + 1) * KN_ROWS)
        kn_ref[rows, :] = jnp.maximum(kn_ref[rows, :], n2)
        vseg = acc[:, KV_WIDTH + hh * HEAD_DIM:KV_WIDTH + (hh + 1) * HEAD_DIM]
        v_ref[hh] = vseg.T.astype(BF16)
    for cb in range(ATTN_WIDTH // PROJ_TN):
        g = proj(COL_GA + cb * PROJ_TN, PROJ_TN)
        ga_ref[:, cb * PROJ_TN:(cb + 1) * PROJ_TN] = g * _sigmoid(g)
    for cb in range(LRU_WIDTH // PROJ_TN):
        acc = proj(COL_XR + cb * PROJ_TN, PROJ_TN)
        for q in range(PROJ_TN // LANES):
            n = cb * (PROJ_TN // LANES) + q
            xr_ref[pl.ds(n, tm, stride=LRU_BLOCKS), :] = acc[:, q * LANES:(q + 1) * LANES]
    for cb in range(LRU_WIDTH // PROJ_TN):
        g = proj(COL_GL + cb * PROJ_TN, PROJ_TN)
        gl_ref[:, cb * PROJ_TN:(cb + 1) * PROJ_TN] = g * _sigmoid(g)


def _in_proj(x2, norm_w, w_in_bf16, q_norm_w, k_norm_w, row_cos, row_sin, col_cos, col_sin):
    S = x2.shape[0]
    tm = PROJ_TM
    assert tm % GRID_W == 0 and S % tm == 0
    const = lambda i: (0, 0)
    est = (2 * tm * D_MODEL * 4 + D_MODEL * IN_WIDTH * 2 + 4 * tm * HEAD_DIM * 4
           + 2 * (tm * ATTN_WIDTH * 2 + 2 * tm * KV_WIDTH * 2 + 3 * tm * ATTN_WIDTH * 4)
           + tm * D_MODEL * 6 + 4 * tm * PROJ_TN * 4)
    return pl.pallas_call(
        _in_proj_kernel,
        grid=(S // tm,),
        in_specs=[
            pl.BlockSpec((tm, D_MODEL), lambda i: (i, 0)),
            pl.BlockSpec((1, D_MODEL), const),
            pl.BlockSpec((D_MODEL, IN_WIDTH), const, pipeline_mode=pl.Buffered(1)),
            pl.BlockSpec((1, HEAD_DIM), const),
            pl.BlockSpec((1, HEAD_DIM), const),
            pl.BlockSpec((S // GRID_W, HEAD_DIM), const),
            pl.BlockSpec((S // GRID_W, HEAD_DIM), const),
            pl.BlockSpec((tm, HEAD_DIM), const),
            pl.BlockSpec((tm, HEAD_DIM), const),
        ],
        out_specs=[
            pl.BlockSpec((N_KV_HEADS, 1, HEAD_DIM, GROUP * tm), lambda i: (0, i, 0, 0)),
            pl.BlockSpec((N_KV_HEADS, tm, HEAD_DIM), lambda i: (0, i, 0)),
            pl.BlockSpec((N_KV_HEADS, HEAD_DIM, tm), lambda i: (0, 0, i)),
            pl.BlockSpec((N_KV_HEADS * KN_ROWS, LANES), const),
            pl.BlockSpec((tm, ATTN_WIDTH), lambda i: (i, 0)),
            pl.BlockSpec((tm * LRU_BLOCKS, LANES), lambda i: (i, 0)),
            pl.BlockSpec((tm, LRU_WIDTH), lambda i: (i, 0)),
        ],
        out_shape=[
            jax.ShapeDtypeStruct((N_KV_HEADS, S // tm, HEAD_DIM, GROUP * tm), BF16),
            jax.ShapeDtypeStruct((N_KV_HEADS, S, HEAD_DIM), BF16),
            jax.ShapeDtypeStruct((N_KV_HEADS, HEAD_DIM, S), BF16),
            jax.ShapeDtypeStruct((N_KV_HEADS * KN_ROWS, LANES), F32),
            jax.ShapeDtypeStruct((S, ATTN_WIDTH), F32),
            jax.ShapeDtypeStruct((S * LRU_BLOCKS, LANES), F32),
            jax.ShapeDtypeStruct((S, LRU_WIDTH), F32),
        ],
        compiler_params=pltpu.CompilerParams(
            dimension_semantics=("arbitrary",), vmem_limit_bytes=_vmem_limit(est)),
        name="in_proj",
    )(x2, norm_w, w_in_bf16, q_norm_w, k_norm_w, row_cos, row_sin, col_cos, col_sin)


ATTN_TQ = PROJ_TM
ATTN_TK = 1024


SOFTMAX_ROWS = 32


def _sublane_allmax(x):
    for shift in (4, 2, 1):
        x = jnp.maximum(x, pltpu.roll(x, shift, 0))
    return x


SAFE_LOGIT_BOUND = 55.0
BOUND_SLACK = 1.0 + 2.0 ** -10
LANE_BLOCK = 256


def _attn_kernel(kn_ref, qt_ref, k_ref, vt_ref, o_ref, m_ref, l_ref, acc_ref, s_buf, tmax_buf,
                 p_buf, alpha_buf, *, tq, tk):
    lanes = GROUP * tq
    n_kv = k_ref.shape[0] // tk
    n_groups = tk // SUBLANES
    chunk = SOFTMAX_ROWS // SUBLANES
    acc_shape = (HEAD_DIM // SUBLANES, SUBLANES, lanes)

    qf = qt_ref[...].astype(F32)
    qn2 = jnp.sum(qf * qf, axis=0, keepdims=True)
    kn = kn_ref[...]
    kn2 = jnp.where(pl.program_id(0) == 0, kn[0:1, :], kn[KN_ROWS:KN_ROWS + 1, :])
    bound = jnp.sqrt(qn2 * jnp.tile(kn2, (1, lanes // LANES))) * BOUND_SLACK
    single_pass = jnp.max(bound) <= SAFE_LOGIT_BOUND

    def kv_rows(j):
        return pl.ds(pl.multiple_of(j * tk, tk), tk)

    def finish():
        l_tot = jnp.sum(l_ref[...], axis=0, keepdims=True)
        out_t = acc_ref[...].reshape(HEAD_DIM, lanes) / l_tot
        for h in range(GROUP):
            o_ref[:, h * HEAD_DIM:(h + 1) * HEAD_DIM] = out_t[:, h * tq:(h + 1) * tq].T

    l_ref[...] = jnp.zeros((SUBLANES, lanes), F32)
    acc_ref[...] = jnp.zeros(acc_shape, F32)

    @pl.when(single_pass)
    def _():
        m_ref[...] = jnp.broadcast_to(bound, (SUBLANES, lanes))

        def probs(j, slot):
            kt = k_ref[kv_rows(j), :]
            for lb in range(lanes // LANE_BLOCK):
                cols = slice(lb * LANE_BLOCK, (lb + 1) * LANE_BLOCK)
                s = jnp.dot(kt, qt_ref[:, cols], preferred_element_type=F32)
                p = jnp.exp2(s.reshape(n_groups, SUBLANES, LANE_BLOCK) - m_ref[:, cols])
                l_ref[:, cols] += jnp.sum(p, axis=0)
                p_buf[slot, :, cols] = p.reshape(tk, LANE_BLOCK).astype(BF16)

        def values(j, slot):
            pv = jnp.dot(vt_ref[:, kv_rows(j)], p_buf[slot], preferred_element_type=F32)
            acc_ref[...] += pv.reshape(acc_shape)

        probs(0, 0)

        def body(t, _):
            j = 2 * t
            probs(j + 1, 1)
            values(j, 0)
            probs(j + 2, 0)
            values(j + 1, 1)
            return 0

        lax.fori_loop(0, n_kv // 2 - 1, body, 0)
        probs(n_kv - 1, 1)
        values(n_kv - 2, 0)
        values(n_kv - 1, 1)
        finish()

    @pl.when(jnp.logical_not(single_pass))
    def _():
        def scores(j, slot):
            s = jnp.dot(k_ref[kv_rows(j), :], qt_ref[...], preferred_element_type=F32)
            s = s.reshape(n_groups, SUBLANES, lanes)
            s_buf[slot] = s
            tmax_buf[slot] = jnp.max(s, axis=0)

        def softmax(slot):
            m_old = m_ref[...]
            m_new = jnp.maximum(m_old, _sublane_allmax(tmax_buf[slot]))
            alpha = jnp.exp2(m_old - m_new)
            psum = jnp.zeros((SUBLANES, lanes), F32)
            for c in range(n_groups // chunk):
                s = s_buf[slot, c * chunk:(c + 1) * chunk]
                p = jnp.exp2(s - m_new)
                psum = psum + jnp.sum(p, axis=0)
                p_buf[slot, c * SOFTMAX_ROWS:(c + 1) * SOFTMAX_ROWS, :] = (
                    p.reshape(SOFTMAX_ROWS, lanes).astype(BF16))
            l_ref[...] = alpha * l_ref[...] + psum
            m_ref[...] = m_new
            alpha_buf[slot] = alpha

        def values(j, slot):
            pv = jnp.dot(vt_ref[:, kv_rows(j)], p_buf[slot], preferred_element_type=F32)
            acc_ref[...] = alpha_buf[slot] * acc_ref[...] + pv.reshape(acc_shape)

        m_ref[...] = jnp.full((SUBLANES, lanes), NEG_BIG, F32)
        p_buf[1] = jnp.zeros((tk, lanes), BF16)
        alpha_buf[1] = jnp.ones((SUBLANES, lanes), F32)
        scores(0, 0)

        def body(t, _):
            for slot in range(2):
                cur = 2 * t + slot
                nxt = jnp.where(cur + 1 < n_kv, cur + 1, 0)
                prv = jnp.maximum(cur - 1, 0)
                scores(nxt, 1 - slot)
                softmax(slot)
                values(prv, 1 - slot)
            return 0

        lax.fori_loop(0, n_kv // 2, body, 0)
        values(n_kv - 1, 1)
        finish()


def _attention(k_norm2, q_t, k, v_t):
    S = k.shape[1]
    tq, tk = ATTN_TQ, ATTN_TK
    assert S % (2 * tk) == 0 and S >= 4 * tk and tk % SOFTMAX_ROWS == 0
    lanes = GROUP * tq
    est = (2 * HEAD_DIM * lanes * 2 + 2 * 2 * S * HEAD_DIM * 2 + 2 * tq * GROUP * HEAD_DIM * 4
           + (HEAD_DIM + 6 * SUBLANES) * lanes * 4 + 2 * tk * lanes * 6 + 2 * tk * lanes * 4)
    return pl.pallas_call(
        functools.partial(_attn_kernel, tq=tq, tk=tk),
        grid=(N_KV_HEADS, S // tq),
        in_specs=[
            pl.BlockSpec((N_KV_HEADS * KN_ROWS, LANES), lambda g, i: (0, 0)),
            pl.BlockSpec((None, None, HEAD_DIM, lanes), lambda g, i: (g, i, 0, 0)),
            pl.BlockSpec((None, S, HEAD_DIM), lambda g, i: (g, 0, 0)),
            pl.BlockSpec((None, HEAD_DIM, S), lambda g, i: (g, 0, 0)),
        ],
        out_specs=pl.BlockSpec((tq, GROUP * HEAD_DIM), lambda g, i: (i, g)),
        out_shape=jax.ShapeDtypeStruct((S, ATTN_WIDTH), F32),
        scratch_shapes=[
            pltpu.VMEM((SUBLANES, lanes), F32),
            pltpu.VMEM((SUBLANES, lanes), F32),
            pltpu.VMEM((HEAD_DIM // SUBLANES, SUBLANES, lanes), F32),
            pltpu.VMEM((2, tk // SUBLANES, SUBLANES, lanes), F32),
            pltpu.VMEM((2, SUBLANES, lanes), F32),
            pltpu.VMEM((2, tk, lanes), BF16),
            pltpu.VMEM((2, SUBLANES, lanes), F32),
        ],
        compiler_params=pltpu.CompilerParams(
            dimension_semantics=("arbitrary", "arbitrary"), vmem_limit_bytes=_vmem_limit(est)),
        name="attention",
    )(k_norm2, q_t, k, v_t)


LRU_T = 256
SCAN_UNROLL = 8
assert LRU_WIDTH == SUBLANES * LANES and LRU_BLOCKS == SUBLANES


def _lru_gates(prev_ref, cur_ref, next_ref, ext_tm, xc_tm, a_tm, b_tm, has_prev, has_next,
               cw_ref, cb_ref, w_ref, bias_ref, lam_ref):
    rows = cur_ref.shape[0]
    T = rows // LRU_BLOCKS
    halo = prev_ref.shape[0]
    ext_tm[0:halo, :] = jnp.where(has_prev, prev_ref[...], 0.0)
    ext_tm[halo:halo + rows, :] = cur_ref[...]
    ext_tm[halo + rows:2 * halo + rows, :] = jnp.where(has_next, next_ref[...], 0.0)
    base = halo - CONV_LEFT * LRU_BLOCKS
    xc = cb_ref[...]
    for j in range(CONV_W):
        start = base + j * LRU_BLOCKS
        tap = ext_tm[start:start + rows, :].reshape(T, LRU_BLOCKS, LRU_BW)
        xc = xc + cw_ref[j] * tap
    xc_tm[...] = xc.reshape(rows, LRU_BW)
    lam = lam_ref[...]
    neg = -lam
    softplus = jnp.maximum(neg, 0.0) + jnp.log1p(jnp.exp(-jnp.abs(neg)))
    decay = -LRU_C * softplus
    for n in range(LRU_BLOCKS):
        cols = slice(n * LRU_BW, (n + 1) * LRU_BW)
        block_n = pl.ds(n, T, stride=LRU_BLOCKS)
        xb = xc_tm[block_n, :]
        z = jnp.dot(xb.astype(BF16), w_ref[n], preferred_element_type=F32)
        r = _sigmoid(z[:, :LRU_BW] + bias_ref[0:1, cols])
        gate_i = _sigmoid(z[:, LRU_BW:] + bias_ref[1:2, cols])
        a = jnp.exp(r * decay[:, cols])
        one_minus_a2 = 1.0 - a * a
        mult = one_minus_a2 * lax.rsqrt(jnp.maximum(one_minus_a2, TINY))
        a_tm[block_n, :] = a
        b_tm[block_n, :] = mult * (gate_i * xb)


def _lru_scan(af_tm, bf_tm, hf_tm, ab_tm, bb_tm, hb_tm, carry_ref):
    T = af_tm.shape[0] // LRU_BLOCKS

    def body(i, carry):
        h_f, h_b = carry
        for u in range(SCAN_UNROLL):
            t = i * SCAN_UNROLL + u
            rf = pl.ds(pl.multiple_of(t * LRU_BLOCKS, LRU_BLOCKS), LRU_BLOCKS)
            rb = pl.ds(pl.multiple_of((T - 1 - t) * LRU_BLOCKS, LRU_BLOCKS), LRU_BLOCKS)
            h_f = af_tm[rf, :] * h_f + bf_tm[rf, :]
            hf_tm[rf, :] = h_f
            h_b = ab_tm[rb, :] * h_b + bb_tm[rb, :]
            hb_tm[rb, :] = h_b
        return h_f, h_b

    h_f, h_b = lax.fori_loop(0, T // SCAN_UNROLL, body, (carry_ref[0], carry_ref[1]))
    carry_ref[0] = h_f
    carry_ref[1] = h_b


def _lru_kernel(fp_ref, fc_ref, fn_ref, bp_ref, bc_ref, bn_ref, cw_ref, cb_ref, w_ref, bias_ref,
                lam_ref, hf_ref, hb_ref, ext_tm, xc_tm, af_tm, bf_tm, hf_tm, ab_tm, bb_tm, hb_tm,
                carry_ref):
    c = pl.program_id(0)
    n_chunks = pl.num_programs(0)
    T = fc_ref.shape[0] // LRU_BLOCKS

    @pl.when(c == 0)
    def _():
        carry_ref[...] = jnp.zeros(carry_ref.shape, F32)

    _lru_gates(fp_ref, fc_ref, fn_ref, ext_tm, xc_tm, af_tm, bf_tm, c > 0, c < n_chunks - 1,
               cw_ref, cb_ref, w_ref.at[0], bias_ref.at[0], lam_ref.at[0])
    _lru_gates(bp_ref, bc_ref, bn_ref, ext_tm, xc_tm, ab_tm, bb_tm, c < n_chunks - 1, c > 0,
               cw_ref, cb_ref, w_ref.at[1], bias_ref.at[1], lam_ref.at[1])
    _lru_scan(af_tm, bf_tm, hf_tm, ab_tm, bb_tm, hb_tm, carry_ref)
    for n in range(LRU_BLOCKS):
        cols = slice(n * LRU_BW, (n + 1) * LRU_BW)
        hf_ref[:, cols] = hf_tm[pl.ds(n, T, stride=LRU_BLOCKS), :]
        hb_ref[:, cols] = hb_tm[pl.ds(n, T, stride=LRU_BLOCKS), :]


LRU_HALO = SUBLANES


def _lru(xr_tm, conv_w_tm, conv_b_tm, w_gate, bias_gate, lam):
    S = xr_tm.shape[0] // LRU_BLOCKS
    T = LRU_T
    assert S % T == 0 and T % SCAN_UNROLL == 0 and T % LRU_HALO == 0
    assert CONV_LEFT <= LRU_HALO and CONV_W - 1 - CONV_LEFT <= LRU_HALO
    time_major = pltpu.VMEM((T * LRU_BLOCKS, LRU_BW), F32)
    n_chunks = S // T
    per = T // LRU_HALO
    last_blk = S // LRU_HALO - 1
    W = LRU_WIDTH
    chunk = (T * LRU_BLOCKS, LRU_BW)
    halo = (LRU_HALO * LRU_BLOCKS, LRU_BW)
    const2 = lambda c: (0, 0)
    est = (2 * 2 * (T * W * 4 + 2 * SUBLANES * W * 4) + 2 * 2 * T * W * 4
           + 2 * 2 * LRU_BLOCKS * LRU_BW * 2 * LRU_BW * 2 + (8 * T + 64) * W * 4 + 6 * T * W * 4)
    return pl.pallas_call(
        _lru_kernel,
        grid=(n_chunks,),
        in_specs=[
            pl.BlockSpec(halo, lambda c: (jnp.maximum(c * per - 1, 0), 0)),
            pl.BlockSpec(chunk, lambda c: (c, 0)),
            pl.BlockSpec(halo, lambda c: (jnp.minimum((c + 1) * per, last_blk), 0)),
            pl.BlockSpec(halo, lambda c: (jnp.maximum((n_chunks - 1 - c) * per - 1, 0), 0)),
            pl.BlockSpec(chunk, lambda c: (n_chunks - 1 - c, 0)),
            pl.BlockSpec(halo, lambda c: (jnp.minimum((n_chunks - c) * per, last_blk), 0)),
            pl.BlockSpec((CONV_W, LRU_BLOCKS, LRU_BW), lambda c: (0, 0, 0)),
            pl.BlockSpec((LRU_BLOCKS, LRU_BW), const2),
            pl.BlockSpec((2, LRU_BLOCKS, LRU_BW, 2 * LRU_BW), lambda c: (0, 0, 0, 0)),
            pl.BlockSpec((2, 2, W), lambda c: (0, 0, 0)),
            pl.BlockSpec((2, 1, W), lambda c: (0, 0, 0)),
        ],
        out_specs=[
            pl.BlockSpec((T, W), lambda c: (c, 0)),
            pl.BlockSpec((T, W), lambda c: (n_chunks - 1 - c, 0)),
        ],
        out_shape=[jax.ShapeDtypeStruct((S, W), F32), jax.ShapeDtypeStruct((S, W), F32)],
        scratch_shapes=[
            pltpu.VMEM(((T + 2 * LRU_HALO) * LRU_BLOCKS, LRU_BW), F32),
            time_major,
            time_major, time_major, time_major,
            time_major, time_major, time_major,
            pltpu.VMEM((2, SUBLANES, LRU_BW), F32),
        ],
        compiler_params=pltpu.CompilerParams(
            dimension_semantics=("arbitrary",), vmem_limit_bytes=_vmem_limit(est)),
        name="rg_lru",
    )(xr_tm, xr_tm, xr_tm, xr_tm, xr_tm, xr_tm, conv_w_tm, conv_b_tm, w_gate, bias_gate, lam)


OUT_TM = 256


def _out_proj_kernel(ao_ref, ga_ref, hf_ref, hb_ref, gl_ref, x_ref, anw_ref, lnw_ref, w_ref,
                     o_ref):
    def normed(v, gain):
        ms = jnp.mean(v * v, axis=-1, keepdims=True)
        return v * lax.rsqrt(ms + EPS) * gain

    ya = (normed(ao_ref[...], anw_ref[...]) * ga_ref[...]).astype(BF16)
    yl = (normed(hf_ref[...] + hb_ref[...], lnw_ref[...]) * gl_ref[...]).astype(BF16)
    acc = jnp.dot(ya, w_ref[0:ATTN_WIDTH, :], preferred_element_type=F32)
    acc = acc + jnp.dot(yl, w_ref[ATTN_WIDTH:MIX_WIDTH, :], preferred_element_type=F32)
    o_ref[...] = x_ref[...] + acc


def _out_proj(attn_out, g_attn, h_fwd, h_bwd, g_lru, x2, attn_norm_w, lru_norm_w, w_out_bf16):
    S = x2.shape[0]
    tm = OUT_TM
    row = lambda i: (i, 0)
    const = lambda i: (0, 0)
    est = (2 * 5 * tm * ATTN_WIDTH * 4 + 2 * 2 * tm * D_MODEL * 4 + 2 * MIX_WIDTH * D_MODEL * 2
           + 4 * tm * D_MODEL * 4)
    return pl.pallas_call(
        _out_proj_kernel,
        grid=(S // tm,),
        in_specs=[
            pl.BlockSpec((tm, ATTN_WIDTH), row),
            pl.BlockSpec((tm, ATTN_WIDTH), row),
            pl.BlockSpec((tm, LRU_WIDTH), row),
            pl.BlockSpec((tm, LRU_WIDTH), row),
            pl.BlockSpec((tm, LRU_WIDTH), row),
            pl.BlockSpec((tm, D_MODEL), row),
            pl.BlockSpec((1, ATTN_WIDTH), const),
            pl.BlockSpec((1, LRU_WIDTH), const),
            pl.BlockSpec((MIX_WIDTH, D_MODEL), const),
        ],
        out_specs=pl.BlockSpec((tm, D_MODEL), row),
        out_shape=jax.ShapeDtypeStruct((S, D_MODEL), F32),
        compiler_params=pltpu.CompilerParams(
            dimension_semantics=("arbitrary",), vmem_limit_bytes=_vmem_limit(est)),
        name="out_proj",
    )(attn_out, g_attn, h_fwd, h_bwd, g_lru, x2, attn_norm_w, lru_norm_w, w_out_bf16)


def _rope_tables(seq_len, tile_rows):
    rows = seq_len // GRID_W
    inv_freq = ROPE_THETA ** (-jnp.arange(ROPE_PAIRS, dtype=F32) / ROPE_PAIRS)
    ang_r = jnp.arange(rows, dtype=F32)[:, None] * inv_freq[None, :]
    ang_c = jnp.arange(GRID_W, dtype=F32)[:, None] * inv_freq[None, :]
    cr, sr, cc, sc = jnp.cos(ang_r), jnp.sin(ang_r), jnp.cos(ang_c), jnp.sin(ang_c)
    zr, zc = jnp.zeros_like(cr), jnp.zeros_like(cc)
    reps = tile_rows // GRID_W
    return (jnp.concatenate([cr, cr, zr, zr], axis=-1),
            jnp.concatenate([-sr, sr, zr, zr], axis=-1),
            jnp.tile(jnp.concatenate([zc, zc, cc, cc], axis=-1), (reps, 1)),
            jnp.tile(jnp.concatenate([zc, zc, -sc, sc], axis=-1), (reps, 1)))


def _layer(x2, norm_w, w_in, q_norm_w, k_norm_w, conv_w, conv_b, lru_wa, lru_ba, lru_wx, lru_bx,
           lru_lambda, attn_norm_w, lru_norm_w, w_out):
    S = x2.shape[0]
    q_t, k, v_t, k_norm2, g_attn, xr, g_lru = _in_proj(
        x2, norm_w.reshape(1, D_MODEL), w_in.astype(BF16), q_norm_w.reshape(1, HEAD_DIM),
        k_norm_w.reshape(1, HEAD_DIM), *_rope_tables(S, PROJ_TM))
    attn_out = _attention(k_norm2, q_t, k, v_t)
    w_gate = jnp.concatenate([lru_wa, lru_wx], axis=-1).astype(BF16)
    bias_gate = jnp.stack([lru_ba, lru_bx], axis=1)
    h_fwd, h_bwd = _lru(xr, conv_w.reshape(CONV_W, LRU_BLOCKS, LRU_BW),
                        conv_b.reshape(LRU_BLOCKS, LRU_BW),
                        w_gate, bias_gate, lru_lambda.reshape(2, 1, LRU_WIDTH))
    return _out_proj(attn_out, g_attn, h_fwd, h_bwd, g_lru, x2,
                     attn_norm_w.reshape(1, ATTN_WIDTH), lru_norm_w.reshape(1, LRU_WIDTH),
                     w_out.astype(BF16))


def kernel(x, norm_w, w_in, q_norm_w, k_norm_w, conv_w, conv_b, lru_wa, lru_ba, lru_wx, lru_bx,
           lru_lambda, attn_norm_w, lru_norm_w, w_out):
    B, S, D = x.shape
    assert D == D_MODEL and S % GRID_W == 0
    outs = [_layer(x[b], norm_w, w_in, q_norm_w, k_norm_w, conv_w, conv_b, lru_wa, lru_ba,
                   lru_wx, lru_bx, lru_lambda, attn_norm_w, lru_norm_w, w_out)
            for b in range(B)]
    return jnp.stack(outs, axis=0) if B > 1 else outs[0][None]
```

```python
import functools
import math

import jax
import jax.numpy as jnp
from jax import lax
from jax.experimental import pallas as pl
from jax.experimental.pallas import tpu as pltpu

F32 = jnp.float32
BF16 = jnp.bfloat16

D_MODEL = 2048
N_HEADS = 8
N_KV_HEADS = 2
GROUP = N_HEADS // N_KV_HEADS
HEAD_DIM = 128
ATTN_WIDTH = N_HEADS * HEAD_DIM
KV_WIDTH = N_KV_HEADS * HEAD_DIM
ROPE_THETA = 10000.0
ROPE_PAIRS = HEAD_DIM // 4
GRID_W = 64
LRU_WIDTH = D_MODEL // 2
LRU_BLOCKS = 8
LRU_BW = LRU_WIDTH // LRU_BLOCKS
LRU_C = 8.0
CONV_W = 4
CONV_LEFT = 2
MIX_WIDTH = ATTN_WIDTH + LRU_WIDTH
IN_WIDTH = 2 * ATTN_WIDTH + 2 * KV_WIDTH + 2 * LRU_WIDTH
EPS = 1e-6

COL_Q = 0
COL_K = ATTN_WIDTH
COL_V = ATTN_WIDTH + KV_WIDTH
COL_GA = ATTN_WIDTH + 2 * KV_WIDTH
COL_XR = COL_GA + ATTN_WIDTH
COL_GL = COL_XR + LRU_WIDTH

SUBLANES = 8
LANES = 128
VMEM_LIMIT_CAP = 60000 * 1024
MIB = 1024 * 1024

NEG_BIG = -1e30
TINY = 1e-30
LOG2_E = math.log2(math.e)


def _vmem_limit(estimate_bytes):
    return int(min(VMEM_LIMIT_CAP, estimate_bytes + 8 * MIB))


def _sigmoid(x):
    return 0.5 * jnp.tanh(0.5 * x) + 0.5


PROJ_TM = 256
PROJ_TN = 512
KN_ROWS = SUBLANES // N_KV_HEADS


def _in_proj_kernel(x_ref, nw_ref, w_ref, qw_ref, kw_ref, rcos_ref, rsin_ref, ccos_ref, csin_ref,
                    q_ref, k_ref, v_ref, kn_ref, ga_ref, xr_ref, gl_ref):
    x = x_ref[...]
    ms = jnp.mean(x * x, axis=-1, keepdims=True)
    inv_rms = lax.rsqrt(ms + EPS)
    h = (x * nw_ref[...]).astype(BF16)

    def rope_table(row_ref, col_ref):
        grid_rows = x_ref.shape[0] // GRID_W
        first = pl.program_id(0) * grid_rows
        lines = [jnp.broadcast_to(row_ref[pl.ds(first + r, 1), :], (GRID_W, HEAD_DIM))
                 for r in range(grid_rows)]
        return jnp.concatenate(lines, axis=0) + col_ref[...]

    cos = rope_table(rcos_ref, ccos_ref)
    sin = rope_table(rsin_ref, csin_ref)
    lane = lax.broadcasted_iota(jnp.int32, (1, HEAD_DIM), 1)
    low_half = (lane % (2 * ROPE_PAIRS)) < ROPE_PAIRS

    def proj(col, width):
        return jnp.dot(h, w_ref[:, col:col + width], preferred_element_type=F32) * inv_rms

    def norm_rope(seg, gain, scale):
        ms_h = jnp.mean(seg * seg, axis=-1, keepdims=True)
        y = seg * lax.rsqrt(ms_h + EPS) * gain
        partner = jnp.where(low_half, pltpu.roll(y, HEAD_DIM - ROPE_PAIRS, 1),
                            pltpu.roll(y, ROPE_PAIRS, 1))
        return (y * cos + partner * sin) * scale

    scale = HEAD_DIM ** -0.5 * LOG2_E
    qw = qw_ref[...]
    kw = kw_ref[...]
    tm = x_ref.shape[0]
    for cb in range(ATTN_WIDTH // PROJ_TN):
        acc = proj(COL_Q + cb * PROJ_TN, PROJ_TN)
        for hh in range(PROJ_TN // HEAD_DIM):
            head = cb * (PROJ_TN // HEAD_DIM) + hh
            seg = acc[:, hh * HEAD_DIM:(hh + 1) * HEAD_DIM]
            qt = norm_rope(seg, qw, scale).T.astype(BF16)
            lane0 = (head % GROUP) * tm
            q_ref[head // GROUP, 0, :, lane0:lane0 + tm] = qt
    acc = proj(COL_K, 2 * KV_WIDTH)

    @pl.when(pl.program_id(0) == 0)
    def _():
        kn_ref[...] = jnp.zeros(kn_ref.shape, F32)

    for hh in range(N_KV_HEADS):
        kb = norm_rope(acc[:, hh * HEAD_DIM:(hh + 1) * HEAD_DIM], kw, 1.0).astype(BF16)
        k_ref[hh] = kb
        kf = kb.astype(F32)
        n2 = jnp.max(jnp.sum(kf * kf, axis=-1, keepdims=True), axis=0, keepdims=True)
        rows = slice(hh * KN_ROWS, (hh + 1) * KN_ROWS)
        kn_ref[rows, :] = jnp.maximum(kn_ref[rows, :], n2)
        vseg = acc[:, KV_WIDTH + hh * HEAD_DIM:KV_WIDTH + (hh + 1) * HEAD_DIM]
        v_ref[hh] = vseg.T.astype(BF16)
    for cb in range(ATTN_WIDTH // PROJ_TN):
        g = proj(COL_GA + cb * PROJ_TN, PROJ_TN)
        ga_ref[:, cb * PROJ_TN:(cb + 1) * PROJ_TN] = (g * _sigmoid(g)).astype(BF16)
    for cb in range(LRU_WIDTH // PROJ_TN):
        acc = proj(COL_XR + cb * PROJ_TN, PROJ_TN)
        for q in range(PROJ_TN // LANES):
            n = cb * (PROJ_TN // LANES) + q
            xr_ref[pl.ds(n, tm, stride=LRU_BLOCKS), :] = acc[:, q * LANES:(q + 1) * LANES]
    for cb in range(LRU_WIDTH // PROJ_TN):
        g = proj(COL_GL + cb * PROJ_TN, PROJ_TN)
        gl_ref[:, cb * PROJ_TN:(cb + 1) * PROJ_TN] = (g * _sigmoid(g)).astype(BF16)


def _in_proj(x2, norm_w, w_in_bf16, q_norm_w, k_norm_w, row_cos, row_sin, col_cos, col_sin):
    S = x2.shape[0]
    tm = PROJ_TM
    assert tm % GRID_W == 0 and S % tm == 0
    const = lambda i: (0, 0)
    est = (2 * tm * D_MODEL * 4 + D_MODEL * IN_WIDTH * 2 + 4 * tm * HEAD_DIM * 4
           + 2 * (3 * tm * ATTN_WIDTH * 2 + 2 * tm * KV_WIDTH * 2 + tm * LRU_WIDTH * 4)
           + tm * D_MODEL * 6 + 4 * tm * PROJ_TN * 4)
    return pl.pallas_call(
        _in_proj_kernel,
        grid=(S // tm,),
        in_specs=[
            pl.BlockSpec((tm, D_MODEL), lambda i: (i, 0)),
            pl.BlockSpec((1, D_MODEL), const),
            pl.BlockSpec((D_MODEL, IN_WIDTH), const, pipeline_mode=pl.Buffered(1)),
            pl.BlockSpec((1, HEAD_DIM), const),
            pl.BlockSpec((1, HEAD_DIM), const),
            pl.BlockSpec((S // GRID_W, HEAD_DIM), const),
            pl.BlockSpec((S // GRID_W, HEAD_DIM), const),
            pl.BlockSpec((tm, HEAD_DIM), const),
            pl.BlockSpec((tm, HEAD_DIM), const),
        ],
        out_specs=[
            pl.BlockSpec((N_KV_HEADS, 1, HEAD_DIM, GROUP * tm), lambda i: (0, i, 0, 0)),
            pl.BlockSpec((N_KV_HEADS, tm, HEAD_DIM), lambda i: (0, i, 0)),
            pl.BlockSpec((N_KV_HEADS, HEAD_DIM, tm), lambda i: (0, 0, i)),
            pl.BlockSpec((N_KV_HEADS * KN_ROWS, LANES), const),
            pl.BlockSpec((tm, ATTN_WIDTH), lambda i: (i, 0)),
            pl.BlockSpec((tm * LRU_BLOCKS, LANES), lambda i: (i, 0)),
            pl.BlockSpec((tm, LRU_WIDTH), lambda i: (i, 0)),
        ],
        out_shape=[
            jax.ShapeDtypeStruct((N_KV_HEADS, S // tm, HEAD_DIM, GROUP * tm), BF16),
            jax.ShapeDtypeStruct((N_KV_HEADS, S, HEAD_DIM), BF16),
            jax.ShapeDtypeStruct((N_KV_HEADS, HEAD_DIM, S), BF16),
            jax.ShapeDtypeStruct((N_KV_HEADS * KN_ROWS, LANES), F32),
            jax.ShapeDtypeStruct((S, ATTN_WIDTH), BF16),
            jax.ShapeDtypeStruct((S * LRU_BLOCKS, LANES), F32),
            jax.ShapeDtypeStruct((S, LRU_WIDTH), BF16),
        ],
        compiler_params=pltpu.CompilerParams(
            dimension_semantics=("arbitrary",), vmem_limit_bytes=_vmem_limit(est)),
        name="in_proj",
    )(x2, norm_w, w_in_bf16, q_norm_w, k_norm_w, row_cos, row_sin, col_cos, col_sin)


ATTN_TQ = PROJ_TM
ATTN_TK = 1024


SOFTMAX_ROWS = 32


def _sublane_allmax(x):
    for shift in (4, 2, 1):
        x = jnp.maximum(x, pltpu.roll(x, shift, 0))
    return x


SAFE_LOGIT_BOUND = 55.0
BOUND_SLACK = 1.0 + 2.0 ** -10
LANE_BLOCK = 256


def _attn_kernel(kn_ref, qt_ref, k_ref, vt_ref, o_ref, m_ref, l_ref, acc_ref, s_buf, tmax_buf,
                 p_buf, alpha_buf, *, tq, tk):
    lanes = GROUP * tq
    n_kv = k_ref.shape[0] // tk
    n_groups = tk // SUBLANES
    chunk = SOFTMAX_ROWS // SUBLANES
    acc_shape = (HEAD_DIM // SUBLANES, SUBLANES, lanes)

    qf = qt_ref[...].astype(F32)
    qn2 = jnp.sum(qf * qf, axis=0, keepdims=True)
    kn = kn_ref[...]
    kn2 = jnp.where(pl.program_id(0) == 0, kn[0:1, :], kn[KN_ROWS:KN_ROWS + 1, :])
    bound = jnp.sqrt(qn2 * jnp.tile(kn2, (1, lanes // LANES))) * BOUND_SLACK
    single_pass = jnp.max(bound) <= SAFE_LOGIT_BOUND

    def kv_rows(j):
        return pl.ds(pl.multiple_of(j * tk, tk), tk)

    def finish():
        l_tot = jnp.sum(l_ref[...], axis=0, keepdims=True)
        out_t = acc_ref[...].reshape(HEAD_DIM, lanes) / l_tot
        for h in range(GROUP):
            o_ref[:, h * HEAD_DIM:(h + 1) * HEAD_DIM] = (
                out_t[:, h * tq:(h + 1) * tq].T.astype(o_ref.dtype))

    l_ref[...] = jnp.zeros((SUBLANES, lanes), F32)
    acc_ref[...] = jnp.zeros(acc_shape, F32)

    @pl.when(single_pass)
    def _():
        m_ref[...] = jnp.broadcast_to(bound, (SUBLANES, lanes))

        def probs(j, slot):
            kt = k_ref[kv_rows(j), :]
            for lb in range(lanes // LANE_BLOCK):
                cols = slice(lb * LANE_BLOCK, (lb + 1) * LANE_BLOCK)
                s = jnp.dot(kt, qt_ref[:, cols], preferred_element_type=F32)
                p = jnp.exp2(s.reshape(n_groups, SUBLANES, LANE_BLOCK) - m_ref[:, cols])
                l_ref[:, cols] += jnp.sum(p, axis=0)
                p_buf[slot, :, cols] = p.reshape(tk, LANE_BLOCK).astype(BF16)

        def values(j, slot):
            pv = jnp.dot(vt_ref[:, kv_rows(j)], p_buf[slot], preferred_element_type=F32)
            acc_ref[...] += pv.reshape(acc_shape)

        probs(0, 0)

        def body(t, _):
            j = 2 * t
            probs(j + 1, 1)
            values(j, 0)
            probs(j + 2, 0)
            values(j + 1, 1)
            return 0

        lax.fori_loop(0, n_kv // 2 - 1, body, 0)
        probs(n_kv - 1, 1)
        values(n_kv - 2, 0)
        values(n_kv - 1, 1)
        finish()

    @pl.when(jnp.logical_not(single_pass))
    def _():
        def scores(j, slot):
            s = jnp.dot(k_ref[kv_rows(j), :], qt_ref[...], preferred_element_type=F32)
            s = s.reshape(n_groups, SUBLANES, lanes)
            s_buf[slot] = s
            tmax_buf[slot] = jnp.max(s, axis=0)

        def softmax(slot):
            m_old = m_ref[...]
            m_new = jnp.maximum(m_old, _sublane_allmax(tmax_buf[slot]))
            alpha = jnp.exp2(m_old - m_new)
            psum = jnp.zeros((SUBLANES, lanes), F32)
            for c in range(n_groups // chunk):
                s = s_buf[slot, c * chunk:(c + 1) * chunk]
                p = jnp.exp2(s - m_new)
                psum = psum + jnp.sum(p, axis=0)
                p_buf[slot, c * SOFTMAX_ROWS:(c + 1) * SOFTMAX_ROWS, :] = (
                    p.reshape(SOFTMAX_ROWS, lanes).astype(BF16))
            l_ref[...] = alpha * l_ref[...] + psum
            m_ref[...] = m_new
            alpha_buf[slot] = alpha

        def values(j, slot):
            pv = jnp.dot(vt_ref[:, kv_rows(j)], p_buf[slot], preferred_element_type=F32)
            acc_ref[...] = alpha_buf[slot] * acc_ref[...] + pv.reshape(acc_shape)

        m_ref[...] = jnp.full((SUBLANES, lanes), NEG_BIG, F32)
        p_buf[1] = jnp.zeros((tk, lanes), BF16)
        alpha_buf[1] = jnp.ones((SUBLANES, lanes), F32)
        scores(0, 0)

        def body(t, _):
            for slot in range(2):
                cur = 2 * t + slot
                nxt = jnp.where(cur + 1 < n_kv, cur + 1, 0)
                prv = jnp.maximum(cur - 1, 0)
                scores(nxt, 1 - slot)
                softmax(slot)
                values(prv, 1 - slot)
            return 0

        lax.fori_loop(0, n_kv // 2, body, 0)
        values(n_kv - 1, 1)
        finish()


def _attention(k_norm2, q_t, k, v_t):
    S = k.shape[1]
    tq, tk = ATTN_TQ, ATTN_TK
    assert S % (2 * tk) == 0 and S >= 4 * tk and tk % SOFTMAX_ROWS == 0
    lanes = GROUP * tq
    est = (2 * HEAD_DIM * lanes * 2 + 2 * 2 * S * HEAD_DIM * 2 + 2 * tq * GROUP * HEAD_DIM * 4
           + (HEAD_DIM + 6 * SUBLANES) * lanes * 4 + 2 * tk * lanes * 6 + 2 * tk * lanes * 4)
    return pl.pallas_call(
        functools.partial(_attn_kernel, tq=tq, tk=tk),
        grid=(N_KV_HEADS, S // tq),
        in_specs=[
            pl.BlockSpec((N_KV_HEADS * KN_ROWS, LANES), lambda g, i: (0, 0)),
            pl.BlockSpec((None, None, HEAD_DIM, lanes), lambda g, i: (g, i, 0, 0)),
            pl.BlockSpec((None, S, HEAD_DIM), lambda g, i: (g, 0, 0)),
            pl.BlockSpec((None, HEAD_DIM, S), lambda g, i: (g, 0, 0)),
        ],
        out_specs=pl.BlockSpec((tq, GROUP * HEAD_DIM), lambda g, i: (i, g)),
        out_shape=jax.ShapeDtypeStruct((S, ATTN_WIDTH), BF16),
        scratch_shapes=[
            pltpu.VMEM((SUBLANES, lanes), F32),
            pltpu.VMEM((SUBLANES, lanes), F32),
            pltpu.VMEM((HEAD_DIM // SUBLANES, SUBLANES, lanes), F32),
            pltpu.VMEM((2, tk // SUBLANES, SUBLANES, lanes), F32),
            pltpu.VMEM((2, SUBLANES, lanes), F32),
            pltpu.VMEM((2, tk, lanes), BF16),
            pltpu.VMEM((2, SUBLANES, lanes), F32),
        ],
        compiler_params=pltpu.CompilerParams(
            dimension_semantics=("arbitrary", "arbitrary"), vmem_limit_bytes=_vmem_limit(est)),
        name="attention",
    )(k_norm2, q_t, k, v_t)


LRU_T = 256
SCAN_UNROLL = 8
assert LRU_WIDTH == SUBLANES * LANES and LRU_BLOCKS == SUBLANES


def _lru_gates(prev_ref, cur_ref, next_ref, ext_tm, xc_tm, a_tm, b_tm, has_prev, has_next,
               cw_ref, cb_ref, w_ref, bias_ref, lam_ref):
    rows = cur_ref.shape[0]
    T = rows // LRU_BLOCKS
    halo = prev_ref.shape[0]
    ext_tm[0:halo, :] = jnp.where(has_prev, prev_ref[...], 0.0)
    ext_tm[halo:halo + rows, :] = cur_ref[...]
    ext_tm[halo + rows:2 * halo + rows, :] = jnp.where(has_next, next_ref[...], 0.0)
    base = halo - CONV_LEFT * LRU_BLOCKS
    xc = cb_ref[...]
    for j in range(CONV_W):
        start = base + j * LRU_BLOCKS
        tap = ext_tm[start:start + rows, :].reshape(T, LRU_BLOCKS, LRU_BW)
        xc = xc + cw_ref[j] * tap
    xc_tm[...] = xc.reshape(rows, LRU_BW)
    lam = lam_ref[...]
    neg = -lam
    softplus = jnp.maximum(neg, 0.0) + jnp.log1p(jnp.exp(-jnp.abs(neg)))
    half_rate = (-0.5 * LRU_C * LOG2_E) * softplus
    for n in range(LRU_BLOCKS):
        cols = slice(n * LRU_BW, (n + 1) * LRU_BW)
        block_n = pl.ds(n, T, stride=LRU_BLOCKS)
        xb = xc_tm[block_n, :]
        z = jnp.dot(xb.astype(BF16), w_ref[n], preferred_element_type=F32)
        tanh_r = jnp.tanh(z[:, :LRU_BW] + bias_ref[0:1, cols])
        tanh_i = jnp.tanh(z[:, LRU_BW:] + bias_ref[1:2, cols])
        rate = half_rate[:, cols]
        a = jnp.exp2(tanh_r * rate + rate)
        one_minus_a2 = 1.0 - a * a
        mult = one_minus_a2 * lax.rsqrt(jnp.maximum(one_minus_a2, TINY))
        a_tm[block_n, :] = a
        b_tm[block_n, :] = mult * ((0.5 * tanh_i + 0.5) * xb)


def _lru_scan(af_tm, bf_tm, hf_tm, ab_tm, bb_tm, hb_tm, carry_ref):
    T = af_tm.shape[0] // LRU_BLOCKS

    def body(i, carry):
        h_f, h_b = carry
        for u in range(SCAN_UNROLL):
            t = i * SCAN_UNROLL + u
            rf = pl.ds(pl.multiple_of(t * LRU_BLOCKS, LRU_BLOCKS), LRU_BLOCKS)
            rb = pl.ds(pl.multiple_of((T - 1 - t) * LRU_BLOCKS, LRU_BLOCKS), LRU_BLOCKS)
            h_f = af_tm[rf, :] * h_f + bf_tm[rf, :]
            hf_tm[rf, :] = h_f
            h_b = ab_tm[rb, :] * h_b + bb_tm[rb, :]
            hb_tm[rb, :] = h_b
        return h_f, h_b

    h_f, h_b = lax.fori_loop(0, T // SCAN_UNROLL, body, (carry_ref[0], carry_ref[1]))
    carry_ref[0] = h_f
    carry_ref[1] = h_b


def _lru_kernel(fp_ref, fc_ref, fn_ref, bp_ref, bc_ref, bn_ref, cw_ref, cb_ref, w_ref, bias_ref,
                lam_ref, hf_ref, hb_ref, ext_tm, xc_tm, af_tm, bf_tm, hf_tm, ab_tm, bb_tm, hb_tm,
                carry_ref):
    c = pl.program_id(0)
    n_chunks = pl.num_programs(0)
    T = fc_ref.shape[0] // LRU_BLOCKS

    @pl.when(c == 0)
    def _():
        carry_ref[...] = jnp.zeros(carry_ref.shape, F32)

    _lru_gates(fp_ref, fc_ref, fn_ref, ext_tm, xc_tm, af_tm, bf_tm, c > 0, c < n_chunks - 1,
               cw_ref, cb_ref, w_ref.at[0], bias_ref.at[0], lam_ref.at[0])
    _lru_gates(bp_ref, bc_ref, bn_ref, ext_tm, xc_tm, ab_tm, bb_tm, c < n_chunks - 1, c > 0,
               cw_ref, cb_ref, w_ref.at[1], bias_ref.at[1], lam_ref.at[1])
    _lru_scan(af_tm, bf_tm, hf_tm, ab_tm, bb_tm, hb_tm, carry_ref)
    for n in range(LRU_BLOCKS):
        cols = slice(n * LRU_BW, (n + 1) * LRU_BW)
        hf_ref[:, cols] = hf_tm[pl.ds(n, T, stride=LRU_BLOCKS), :].astype(hf_ref.dtype)
        hb_ref[:, cols] = hb_tm[pl.ds(n, T, stride=LRU_BLOCKS), :].astype(hb_ref.dtype)


LRU_HALO = SUBLANES


def _lru(xr_tm, conv_w_tm, conv_b_tm, w_gate, bias_gate, lam):
    S = xr_tm.shape[0] // LRU_BLOCKS
    T = LRU_T
    assert S % T == 0 and T % SCAN_UNROLL == 0 and T % LRU_HALO == 0
    assert CONV_LEFT <= LRU_HALO and CONV_W - 1 - CONV_LEFT <= LRU_HALO
    time_major = pltpu.VMEM((T * LRU_BLOCKS, LRU_BW), F32)
    n_chunks = S // T
    per = T // LRU_HALO
    last_blk = S // LRU_HALO - 1
    W = LRU_WIDTH
    chunk = (T * LRU_BLOCKS, LRU_BW)
    halo = (LRU_HALO * LRU_BLOCKS, LRU_BW)
    const2 = lambda c: (0, 0)
    est = (2 * 2 * (T * W * 4 + 2 * SUBLANES * W * 4) + 2 * 2 * T * W * 4
           + 2 * 2 * LRU_BLOCKS * LRU_BW * 2 * LRU_BW * 2 + (8 * T + 64) * W * 4 + 6 * T * W * 4)
    return pl.pallas_call(
        _lru_kernel,
        grid=(n_chunks,),
        in_specs=[
            pl.BlockSpec(halo, lambda c: (jnp.maximum(c * per - 1, 0), 0)),
            pl.BlockSpec(chunk, lambda c: (c, 0)),
            pl.BlockSpec(halo, lambda c: (jnp.minimum((c + 1) * per, last_blk), 0)),
            pl.BlockSpec(halo, lambda c: (jnp.maximum((n_chunks - 1 - c) * per - 1, 0), 0)),
            pl.BlockSpec(chunk, lambda c: (n_chunks - 1 - c, 0)),
            pl.BlockSpec(halo, lambda c: (jnp.minimum((n_chunks - c) * per, last_blk), 0)),
            pl.BlockSpec((CONV_W, LRU_BLOCKS, LRU_BW), lambda c: (0, 0, 0)),
            pl.BlockSpec((LRU_BLOCKS, LRU_BW), const2),
            pl.BlockSpec((2, LRU_BLOCKS, LRU_BW, 2 * LRU_BW), lambda c: (0, 0, 0, 0)),
            pl.BlockSpec((2, 2, W), lambda c: (0, 0, 0)),
            pl.BlockSpec((2, 1, W), lambda c: (0, 0, 0)),
        ],
        out_specs=[
            pl.BlockSpec((T, W), lambda c: (c, 0)),
            pl.BlockSpec((T, W), lambda c: (n_chunks - 1 - c, 0)),
        ],
        out_shape=[jax.ShapeDtypeStruct((S, W), BF16), jax.ShapeDtypeStruct((S, W), BF16)],
        scratch_shapes=[
            pltpu.VMEM(((T + 2 * LRU_HALO) * LRU_BLOCKS, LRU_BW), F32),
            time_major,
            time_major, time_major, time_major,
            time_major, time_major, time_major,
            pltpu.VMEM((2, SUBLANES, LRU_BW), F32),
        ],
        compiler_params=pltpu.CompilerParams(
            dimension_semantics=("arbitrary",), vmem_limit_bytes=_vmem_limit(est)),
        name="rg_lru",
    )(xr_tm, xr_tm, xr_tm, xr_tm, xr_tm, xr_tm, conv_w_tm, conv_b_tm, w_gate, bias_gate, lam)


OUT_TM = 256


def _out_proj_kernel(ao_ref, ga_ref, hf_ref, hb_ref, gl_ref, x_ref, anw_ref, lnw_ref, w_ref,
                     o_ref):
    def inv_rms(v):
        return lax.rsqrt(jnp.mean(v * v, axis=-1, keepdims=True) + EPS)

    ao = ao_ref[...].astype(F32)
    hl = hf_ref[...].astype(F32) + hb_ref[...].astype(F32)
    ya = (ao * anw_ref[...] * ga_ref[...].astype(F32)).astype(BF16)
    yl = (hl * lnw_ref[...] * gl_ref[...].astype(F32)).astype(BF16)
    mix_a = jnp.dot(ya, w_ref[0:ATTN_WIDTH, :], preferred_element_type=F32) * inv_rms(ao)
    mix_l = jnp.dot(yl, w_ref[ATTN_WIDTH:MIX_WIDTH, :], preferred_element_type=F32) * inv_rms(hl)
    o_ref[...] = x_ref[...] + mix_a + mix_l


def _out_proj(attn_out, g_attn, h_fwd, h_bwd, g_lru, x2, attn_norm_w, lru_norm_w, w_out_bf16):
    S = x2.shape[0]
    tm = OUT_TM
    row = lambda i: (i, 0)
    const = lambda i: (0, 0)
    est = (2 * 5 * tm * ATTN_WIDTH * 2 + 2 * 2 * tm * D_MODEL * 4 + 2 * MIX_WIDTH * D_MODEL * 2
           + 4 * tm * D_MODEL * 4)
    return pl.pallas_call(
        _out_proj_kernel,
        grid=(S // tm,),
        in_specs=[
            pl.BlockSpec((tm, ATTN_WIDTH), row),
            pl.BlockSpec((tm, ATTN_WIDTH), row),
            pl.BlockSpec((tm, LRU_WIDTH), row),
            pl.BlockSpec((tm, LRU_WIDTH), row),
            pl.BlockSpec((tm, LRU_WIDTH), row),
            pl.BlockSpec((tm, D_MODEL), row),
            pl.BlockSpec((1, ATTN_WIDTH), const),
            pl.BlockSpec((1, LRU_WIDTH), const),
            pl.BlockSpec((MIX_WIDTH, D_MODEL), const),
        ],
        out_specs=pl.BlockSpec((tm, D_MODEL), row),
        out_shape=jax.ShapeDtypeStruct((S, D_MODEL), F32),
        compiler_params=pltpu.CompilerParams(
            dimension_semantics=("arbitrary",), vmem_limit_bytes=_vmem_limit(est)),
        name="out_proj",
    )(attn_out, g_attn, h_fwd, h_bwd, g_lru, x2, attn_norm_w, lru_norm_w, w_out_bf16)


def _rope_tables(seq_len, tile_rows):
    rows = seq_len // GRID_W
    inv_freq = ROPE_THETA ** (-jnp.arange(ROPE_PAIRS, dtype=F32) / ROPE_PAIRS)
    ang_r = jnp.arange(rows, dtype=F32)[:, None] * inv_freq[None, :]
    ang_c = jnp.arange(GRID_W, dtype=F32)[:, None] * inv_freq[None, :]
    cr, sr, cc, sc = jnp.cos(ang_r), jnp.sin(ang_r), jnp.cos(ang_c), jnp.sin(ang_c)
    zr, zc = jnp.zeros_like(cr), jnp.zeros_like(cc)
    reps = tile_rows // GRID_W
    return (jnp.concatenate([cr, cr, zr, zr], axis=-1),
            jnp.concatenate([-sr, sr, zr, zr], axis=-1),
            jnp.tile(jnp.concatenate([zc, zc, cc, cc], axis=-1), (reps, 1)),
            jnp.tile(jnp.concatenate([zc, zc, -sc, sc], axis=-1), (reps, 1)))


def _layer(x2, norm_w, w_in, q_norm_w, k_norm_w, conv_w, conv_b, lru_wa, lru_ba, lru_wx, lru_bx,
           lru_lambda, attn_norm_w, lru_norm_w, w_out):
    S = x2.shape[0]
    q_t, k, v_t, k_norm2, g_attn, xr, g_lru = _in_proj(
        x2, norm_w.reshape(1, D_MODEL), w_in.astype(BF16), q_norm_w.reshape(1, HEAD_DIM),
        k_norm_w.reshape(1, HEAD_DIM), *_rope_tables(S, PROJ_TM))
    attn_out = _attention(k_norm2, q_t, k, v_t)
    w_gate = (0.5 * jnp.concatenate([lru_wa, lru_wx], axis=-1)).astype(BF16)
    bias_gate = 0.5 * jnp.stack([lru_ba, lru_bx], axis=1)
    h_fwd, h_bwd = _lru(xr, conv_w.reshape(CONV_W, LRU_BLOCKS, LRU_BW),
                        conv_b.reshape(LRU_BLOCKS, LRU_BW),
                        w_gate, bias_gate, lru_lambda.reshape(2, 1, LRU_WIDTH))
    return _out_proj(attn_out, g_attn, h_fwd, h_bwd, g_lru, x2,
                     attn_norm_w.reshape(1, ATTN_WIDTH), lru_norm_w.reshape(1, LRU_WIDTH),
                     w_out.astype(BF16))


def kernel(x, norm_w, w_in, q_norm_w, k_norm_w, conv_w, conv_b, lru_wa, lru_ba, lru_wx, lru_bx,
           lru_lambda, attn_norm_w, lru_norm_w, w_out):
    B, S, D = x.shape
    assert D == D_MODEL and S % GRID_W == 0
    outs = [_layer(x[b], norm_w, w_in, q_norm_w, k_norm_w, conv_w, conv_b, lru_wa, lru_ba,
                   lru_wx, lru_bx, lru_lambda, attn_norm_w, lru_norm_w, w_out)
            for b in range(B)]
    return jnp.stack(outs, axis=0) if B > 1 else outs[0][None]
```

```python
import functools
import math

import jax
import jax.numpy as jnp
from jax import lax
from jax.experimental import pallas as pl
from jax.experimental.pallas import tpu as pltpu

F32 = jnp.float32
BF16 = jnp.bfloat16

D_MODEL = 2048
N_HEADS = 8
N_KV_HEADS = 2
GROUP = N_HEADS // N_KV_HEADS
HEAD_DIM = 128
ATTN_WIDTH = N_HEADS * HEAD_DIM
KV_WIDTH = N_KV_HEADS * HEAD_DIM
ROPE_THETA = 10000.0
ROPE_PAIRS = HEAD_DIM // 4
GRID_W = 64
LRU_WIDTH = D_MODEL // 2
LRU_BLOCKS = 8
LRU_BW = LRU_WIDTH // LRU_BLOCKS
LRU_C = 8.0
CONV_W = 4
CONV_LEFT = 2
MIX_WIDTH = ATTN_WIDTH + LRU_WIDTH
IN_WIDTH = 2 * ATTN_WIDTH + 2 * KV_WIDTH + 2 * LRU_WIDTH
EPS = 1e-6

COL_Q = 0
COL_K = ATTN_WIDTH
COL_V = ATTN_WIDTH + KV_WIDTH
COL_GA = ATTN_WIDTH + 2 * KV_WIDTH
COL_XR = COL_GA + ATTN_WIDTH
COL_GL = COL_XR + LRU_WIDTH

SUBLANES = 8
LANES = 128
VMEM_LIMIT_CAP = 60000 * 1024
MIB = 1024 * 1024

NEG_BIG = -1e30
TINY = 1e-30
LOG2_E = math.log2(math.e)


def _vmem_limit(estimate_bytes):
    return int(min(VMEM_LIMIT_CAP, estimate_bytes + 8 * MIB))


def _sigmoid(x):
    return 0.5 * jnp.tanh(0.5 * x) + 0.5


PROJ_TM = 256
PROJ_TN = 512
KN_ROWS = SUBLANES // N_KV_HEADS


def _in_proj_kernel(x_ref, nw_ref, w_ref, qw_ref, kw_ref, rcos_ref, rsin_ref, ccos_ref, csin_ref,
                    q_ref, k_ref, v_ref, kn_ref, ga_ref, xr_ref, gl_ref):
    x = x_ref[...]
    ms = jnp.mean(x * x, axis=-1, keepdims=True)
    inv_rms = lax.rsqrt(ms + EPS)
    h = (x * nw_ref[...]).astype(BF16)

    def rope_table(row_ref, col_ref):
        grid_rows = x_ref.shape[0] // GRID_W
        first = pl.program_id(0) * grid_rows
        lines = [jnp.broadcast_to(row_ref[pl.ds(first + r, 1), :], (GRID_W, HEAD_DIM))
                 for r in range(grid_rows)]
        return jnp.concatenate(lines, axis=0) + col_ref[...]

    cos = rope_table(rcos_ref, ccos_ref)
    sin = rope_table(rsin_ref, csin_ref)
    lane = lax.broadcasted_iota(jnp.int32, (1, HEAD_DIM), 1)
    low_half = (lane % (2 * ROPE_PAIRS)) < ROPE_PAIRS

    def proj(col, width):
        return jnp.dot(h, w_ref[:, col:col + width], preferred_element_type=F32) * inv_rms

    def norm_rope(seg, gain, scale):
        ms_h = jnp.mean(seg * seg, axis=-1, keepdims=True)
        y = seg * lax.rsqrt(ms_h + EPS) * gain
        partner = jnp.where(low_half, pltpu.roll(y, HEAD_DIM - ROPE_PAIRS, 1),
                            pltpu.roll(y, ROPE_PAIRS, 1))
        return (y * cos + partner * sin) * scale

    scale = HEAD_DIM ** -0.5 * LOG2_E
    qw = qw_ref[...]
    kw = kw_ref[...]
    tm = x_ref.shape[0]
    for cb in range(ATTN_WIDTH // PROJ_TN):
        acc = proj(COL_Q + cb * PROJ_TN, PROJ_TN)
        for hh in range(PROJ_TN // HEAD_DIM):
            head = cb * (PROJ_TN // HEAD_DIM) + hh
            seg = acc[:, hh * HEAD_DIM:(hh + 1) * HEAD_DIM]
            qt = norm_rope(seg, qw, scale).T.astype(BF16)
            lane0 = (head % GROUP) * tm
            q_ref[head // GROUP, 0, :, lane0:lane0 + tm] = qt
    acc = proj(COL_K, 2 * KV_WIDTH)

    @pl.when(pl.program_id(0) == 0)
    def _():
        kn_ref[...] = jnp.zeros(kn_ref.shape, F32)

    for hh in range(N_KV_HEADS):
        kb = norm_rope(acc[:, hh * HEAD_DIM:(hh + 1) * HEAD_DIM], kw, 1.0).astype(BF16)
        k_ref[hh] = kb
        kf = kb.astype(F32)
        n2 = jnp.max(jnp.sum(kf * kf, axis=-1, keepdims=True), axis=0, keepdims=True)
        rows = slice(hh * KN_ROWS, (hh + 1) * KN_ROWS)
        kn_ref[rows, :] = jnp.maximum(kn_ref[rows, :], n2)
        vseg = acc[:, KV_WIDTH + hh * HEAD_DIM:KV_WIDTH + (hh + 1) * HEAD_DIM]
        v_ref[hh] = vseg.T.astype(BF16)
    for cb in range(ATTN_WIDTH // PROJ_TN):
        g = proj(COL_GA + cb * PROJ_TN, PROJ_TN)
        ga_ref[:, cb * PROJ_TN:(cb + 1) * PROJ_TN] = (g * _sigmoid(g)).astype(BF16)
    for cb in range(LRU_WIDTH // PROJ_TN):
        acc = proj(COL_XR + cb * PROJ_TN, PROJ_TN)
        for q in range(PROJ_TN // LANES):
            n = cb * (PROJ_TN // LANES) + q
            xr_ref[pl.ds(n, tm, stride=LRU_BLOCKS), :] = acc[:, q * LANES:(q + 1) * LANES]
    for cb in range(LRU_WIDTH // PROJ_TN):
        g = proj(COL_GL + cb * PROJ_TN, PROJ_TN)
        gl_ref[:, cb * PROJ_TN:(cb + 1) * PROJ_TN] = (g * _sigmoid(g)).astype(BF16)


def _in_proj(x2, norm_w, w_in_bf16, q_norm_w, k_norm_w, row_cos, row_sin, col_cos, col_sin):
    S = x2.shape[0]
    tm = PROJ_TM
    assert tm % GRID_W == 0 and S % tm == 0
    const = lambda i: (0, 0)
    est = (2 * tm * D_MODEL * 4 + D_MODEL * IN_WIDTH * 2 + 4 * tm * HEAD_DIM * 4
           + 2 * (3 * tm * ATTN_WIDTH * 2 + 2 * tm * KV_WIDTH * 2 + tm * LRU_WIDTH * 4)
           + tm * D_MODEL * 6 + 4 * tm * PROJ_TN * 4)
    return pl.pallas_call(
        _in_proj_kernel,
        grid=(S // tm,),
        in_specs=[
            pl.BlockSpec((tm, D_MODEL), lambda i: (i, 0)),
            pl.BlockSpec((1, D_MODEL), const),
            pl.BlockSpec((D_MODEL, IN_WIDTH), const, pipeline_mode=pl.Buffered(1)),
            pl.BlockSpec((1, HEAD_DIM), const),
            pl.BlockSpec((1, HEAD_DIM), const),
            pl.BlockSpec((S // GRID_W, HEAD_DIM), const),
            pl.BlockSpec((S // GRID_W, HEAD_DIM), const),
            pl.BlockSpec((tm, HEAD_DIM), const),
            pl.BlockSpec((tm, HEAD_DIM), const),
        ],
        out_specs=[
            pl.BlockSpec((N_KV_HEADS, 1, HEAD_DIM, GROUP * tm), lambda i: (0, i, 0, 0)),
            pl.BlockSpec((N_KV_HEADS, tm, HEAD_DIM), lambda i: (0, i, 0)),
            pl.BlockSpec((N_KV_HEADS, HEAD_DIM, tm), lambda i: (0, 0, i)),
            pl.BlockSpec((N_KV_HEADS * KN_ROWS, LANES), const),
            pl.BlockSpec((tm, ATTN_WIDTH), lambda i: (i, 0)),
            pl.BlockSpec((tm * LRU_BLOCKS, LANES), lambda i: (i, 0)),
            pl.BlockSpec((tm, LRU_WIDTH), lambda i: (i, 0)),
        ],
        out_shape=[
            jax.ShapeDtypeStruct((N_KV_HEADS, S // tm, HEAD_DIM, GROUP * tm), BF16),
            jax.ShapeDtypeStruct((N_KV_HEADS, S, HEAD_DIM), BF16),
            jax.ShapeDtypeStruct((N_KV_HEADS, HEAD_DIM, S), BF16),
            jax.ShapeDtypeStruct((N_KV_HEADS * KN_ROWS, LANES), F32),
            jax.ShapeDtypeStruct((S, ATTN_WIDTH), BF16),
            jax.ShapeDtypeStruct((S * LRU_BLOCKS, LANES), F32),
            jax.ShapeDtypeStruct((S, LRU_WIDTH), BF16),
        ],
        compiler_params=pltpu.CompilerParams(
            dimension_semantics=("arbitrary",), vmem_limit_bytes=_vmem_limit(est)),
        name="in_proj",
    )(x2, norm_w, w_in_bf16, q_norm_w, k_norm_w, row_cos, row_sin, col_cos, col_sin)


ATTN_Q_TILES = 2
ATTN_TK = 1024


SOFTMAX_ROWS = 32


def _sublane_allmax(x):
    for shift in (4, 2, 1):
        x = jnp.maximum(x, pltpu.roll(x, shift, 0))
    return x


SAFE_LOGIT_BOUND = 55.0
BOUND_SLACK = 1.0 + 2.0 ** -10
LANE_BLOCK = 256


def _attn_kernel(kn_ref, q_tiles_ref, k_ref, vt_ref, o_ref, qt_ref, m_ref, l_ref, acc_ref, s_buf,
                 tmax_buf, p_buf, alpha_buf, *, tk):
    n_tiles, _, tile_lanes = q_tiles_ref.shape
    tile_rows = tile_lanes // GROUP
    lanes = n_tiles * tile_lanes
    n_kv = k_ref.shape[0] // tk
    n_groups = tk // SUBLANES
    chunk = SOFTMAX_ROWS // SUBLANES
    acc_shape = (HEAD_DIM // SUBLANES, SUBLANES, lanes)

    for t in range(n_tiles):
        qt_ref[:, t * tile_lanes:(t + 1) * tile_lanes] = q_tiles_ref[t]
    qf = qt_ref[...].astype(F32)
    qn2 = jnp.sum(qf * qf, axis=0, keepdims=True)
    kn = kn_ref[...]
    kn2 = jnp.where(pl.program_id(0) == 0, kn[0:1, :], kn[KN_ROWS:KN_ROWS + 1, :])
    bound = jnp.sqrt(qn2 * jnp.tile(kn2, (1, lanes // LANES))) * BOUND_SLACK
    single_pass = jnp.max(bound) <= SAFE_LOGIT_BOUND

    def kv_rows(j):
        return pl.ds(pl.multiple_of(j * tk, tk), tk)

    def finish():
        l_tot = jnp.sum(l_ref[...], axis=0, keepdims=True)
        out_t = acc_ref[...].reshape(HEAD_DIM, lanes) / l_tot
        for t in range(n_tiles):
            for h in range(GROUP):
                lane0 = t * tile_lanes + h * tile_rows
                o_ref[t * tile_rows:(t + 1) * tile_rows, h * HEAD_DIM:(h + 1) * HEAD_DIM] = (
                    out_t[:, lane0:lane0 + tile_rows].T.astype(o_ref.dtype))

    l_ref[...] = jnp.zeros((SUBLANES, lanes), F32)
    acc_ref[...] = jnp.zeros(acc_shape, F32)

    @pl.when(single_pass)
    def _():
        m_ref[...] = jnp.broadcast_to(bound, (SUBLANES, lanes))

        def probs(j, slot):
            kt = k_ref[kv_rows(j), :]
            for lb in range(lanes // LANE_BLOCK):
                cols = slice(lb * LANE_BLOCK, (lb + 1) * LANE_BLOCK)
                s = jnp.dot(kt, qt_ref[:, cols], preferred_element_type=F32)
                p = jnp.exp2(s.reshape(n_groups, SUBLANES, LANE_BLOCK) - m_ref[:, cols])
                l_ref[:, cols] += jnp.sum(p, axis=0)
                p_buf[slot, :, cols] = p.reshape(tk, LANE_BLOCK).astype(BF16)

        def values(j, slot):
            pv = jnp.dot(vt_ref[:, kv_rows(j)], p_buf[slot], preferred_element_type=F32)
            acc_ref[...] += pv.reshape(acc_shape)

        probs(0, 0)

        def body(t, _):
            j = 2 * t
            probs(j + 1, 1)
            values(j, 0)
            probs(j + 2, 0)
            values(j + 1, 1)
            return 0

        lax.fori_loop(0, n_kv // 2 - 1, body, 0)
        probs(n_kv - 1, 1)
        values(n_kv - 2, 0)
        values(n_kv - 1, 1)
        finish()

    @pl.when(jnp.logical_not(single_pass))
    def _():
        def scores(j, slot):
            s = jnp.dot(k_ref[kv_rows(j), :], qt_ref[...], preferred_element_type=F32)
            s = s.reshape(n_groups, SUBLANES, lanes)
            s_buf[slot] = s
            tmax_buf[slot] = jnp.max(s, axis=0)

        def softmax(slot):
            m_old = m_ref[...]
            m_new = jnp.maximum(m_old, _sublane_allmax(tmax_buf[slot]))
            alpha = jnp.exp2(m_old - m_new)
            psum = jnp.zeros((SUBLANES, lanes), F32)
            for c in range(n_groups // chunk):
                s = s_buf[slot, c * chunk:(c + 1) * chunk]
                p = jnp.exp2(s - m_new)
                psum = psum + jnp.sum(p, axis=0)
                p_buf[slot, c * SOFTMAX_ROWS:(c + 1) * SOFTMAX_ROWS, :] = (
                    p.reshape(SOFTMAX_ROWS, lanes).astype(BF16))
            l_ref[...] = alpha * l_ref[...] + psum
            m_ref[...] = m_new
            alpha_buf[slot] = alpha

        def values(j, slot):
            pv = jnp.dot(vt_ref[:, kv_rows(j)], p_buf[slot], preferred_element_type=F32)
            acc_ref[...] = alpha_buf[slot] * acc_ref[...] + pv.reshape(acc_shape)

        m_ref[...] = jnp.full((SUBLANES, lanes), NEG_BIG, F32)
        p_buf[1] = jnp.zeros((tk, lanes), BF16)
        alpha_buf[1] = jnp.ones((SUBLANES, lanes), F32)
        scores(0, 0)

        def body(t, _):
            for slot in range(2):
                cur = 2 * t + slot
                nxt = jnp.where(cur + 1 < n_kv, cur + 1, 0)
                prv = jnp.maximum(cur - 1, 0)
                scores(nxt, 1 - slot)
                softmax(slot)
                values(prv, 1 - slot)
            return 0

        lax.fori_loop(0, n_kv // 2, body, 0)
        values(n_kv - 1, 1)
        finish()


def _attention(k_norm2, q_t, k, v_t):
    S = k.shape[1]
    tk = ATTN_TK
    n_tiles = ATTN_Q_TILES
    tile_lanes = q_t.shape[-1]
    tq = n_tiles * (tile_lanes // GROUP)
    assert S % (2 * tk) == 0 and S >= 4 * tk and tk % SOFTMAX_ROWS == 0 and S % tq == 0
    lanes = n_tiles * tile_lanes
    est = (3 * HEAD_DIM * lanes * 2 + 2 * 2 * S * HEAD_DIM * 2 + 2 * tq * GROUP * HEAD_DIM * 2
           + (HEAD_DIM + 6 * SUBLANES) * lanes * 4 + 2 * tk * lanes * 6 + 2 * tk * lanes * 4)
    return pl.pallas_call(
        functools.partial(_attn_kernel, tk=tk),
        grid=(N_KV_HEADS, S // tq),
        in_specs=[
            pl.BlockSpec((N_KV_HEADS * KN_ROWS, LANES), lambda g, i: (0, 0)),
            pl.BlockSpec((None, n_tiles, HEAD_DIM, tile_lanes), lambda g, i: (g, i, 0, 0)),
            pl.BlockSpec((None, S, HEAD_DIM), lambda g, i: (g, 0, 0)),
            pl.BlockSpec((None, HEAD_DIM, S), lambda g, i: (g, 0, 0)),
        ],
        out_specs=pl.BlockSpec((tq, GROUP * HEAD_DIM), lambda g, i: (i, g)),
        out_shape=jax.ShapeDtypeStruct((S, ATTN_WIDTH), BF16),
        scratch_shapes=[
            pltpu.VMEM((HEAD_DIM, lanes), BF16),
            pltpu.VMEM((SUBLANES, lanes), F32),
            pltpu.VMEM((SUBLANES, lanes), F32),
            pltpu.VMEM((HEAD_DIM // SUBLANES, SUBLANES, lanes), F32),
            pltpu.VMEM((2, tk // SUBLANES, SUBLANES, lanes), F32),
            pltpu.VMEM((2, SUBLANES, lanes), F32),
            pltpu.VMEM((2, tk, lanes), BF16),
            pltpu.VMEM((2, SUBLANES, lanes), F32),
        ],
        compiler_params=pltpu.CompilerParams(
            dimension_semantics=("arbitrary", "arbitrary"), vmem_limit_bytes=_vmem_limit(est)),
        name="attention",
    )(k_norm2, q_t, k, v_t)


LRU_T = 256
SCAN_UNROLL = 8
assert LRU_WIDTH == SUBLANES * LANES and LRU_BLOCKS == SUBLANES


def _lru_gates(prev_ref, cur_ref, next_ref, ext_tm, xc_tm, a_tm, b_tm, has_prev, has_next,
               cw_ref, cb_ref, w_ref, bias_ref, lam_ref):
    rows = cur_ref.shape[0]
    T = rows // LRU_BLOCKS
    halo = prev_ref.shape[0]
    ext_tm[0:halo, :] = jnp.where(has_prev, prev_ref[...], 0.0)
    ext_tm[halo:halo + rows, :] = cur_ref[...]
    ext_tm[halo + rows:2 * halo + rows, :] = jnp.where(has_next, next_ref[...], 0.0)
    base = halo - CONV_LEFT * LRU_BLOCKS
    xc = cb_ref[...]
    for j in range(CONV_W):
        start = base + j * LRU_BLOCKS
        tap = ext_tm[start:start + rows, :].reshape(T, LRU_BLOCKS, LRU_BW)
        xc = xc + cw_ref[j] * tap
    xc_tm[...] = xc.reshape(rows, LRU_BW)
    lam = lam_ref[...]
    neg = -lam
    softplus = jnp.maximum(neg, 0.0) + jnp.log1p(jnp.exp(-jnp.abs(neg)))
    half_rate = (-0.5 * LRU_C * LOG2_E) * softplus
    for n in range(LRU_BLOCKS):
        cols = slice(n * LRU_BW, (n + 1) * LRU_BW)
        block_n = pl.ds(n, T, stride=LRU_BLOCKS)
        xb = xc_tm[block_n, :]
        z = jnp.dot(xb.astype(BF16), w_ref[n], preferred_element_type=F32)
        tanh_r = jnp.tanh(z[:, :LRU_BW] + bias_ref[0:1, cols])
        tanh_i = jnp.tanh(z[:, LRU_BW:] + bias_ref[1:2, cols])
        rate = half_rate[:, cols]
        a = jnp.exp2(tanh_r * rate + rate)
        one_minus_a2 = 1.0 - a * a
        mult = one_minus_a2 * lax.rsqrt(jnp.maximum(one_minus_a2, TINY))
        a_tm[block_n, :] = a
        b_tm[block_n, :] = mult * ((0.5 * tanh_i + 0.5) * xb)


def _lru_scan(af_tm, bf_tm, hf_tm, ab_tm, bb_tm, hb_tm, carry_ref):
    T = af_tm.shape[0] // LRU_BLOCKS

    def body(i, carry):
        h_f, h_b = carry
        for u in range(SCAN_UNROLL):
            t = i * SCAN_UNROLL + u
            rf = pl.ds(pl.multiple_of(t * LRU_BLOCKS, LRU_BLOCKS), LRU_BLOCKS)
            rb = pl.ds(pl.multiple_of((T - 1 - t) * LRU_BLOCKS, LRU_BLOCKS), LRU_BLOCKS)
            h_f = af_tm[rf, :] * h_f + bf_tm[rf, :]
            hf_tm[rf, :] = h_f
            h_b = ab_tm[rb, :] * h_b + bb_tm[rb, :]
            hb_tm[rb, :] = h_b
        return h_f, h_b

    h_f, h_b = lax.fori_loop(0, T // SCAN_UNROLL, body, (carry_ref[0], carry_ref[1]))
    carry_ref[0] = h_f
    carry_ref[1] = h_b


def _lru_kernel(fp_ref, fc_ref, fn_ref, bp_ref, bc_ref, bn_ref, cw_ref, cb_ref, w_ref, bias_ref,
                lam_ref, hf_ref, hb_ref, ext_tm, xc_tm, af_tm, bf_tm, hf_tm, ab_tm, bb_tm, hb_tm,
                carry_ref):
    c = pl.program_id(0)
    n_chunks = pl.num_programs(0)
    T = fc_ref.shape[0] // LRU_BLOCKS

    @pl.when(c == 0)
    def _():
        carry_ref[...] = jnp.zeros(carry_ref.shape, F32)

    _lru_gates(fp_ref, fc_ref, fn_ref, ext_tm, xc_tm, af_tm, bf_tm, c > 0, c < n_chunks - 1,
               cw_ref, cb_ref, w_ref.at[0], bias_ref.at[0], lam_ref.at[0])
    _lru_gates(bp_ref, bc_ref, bn_ref, ext_tm, xc_tm, ab_tm, bb_tm, c < n_chunks - 1, c > 0,
               cw_ref, cb_ref, w_ref.at[1], bias_ref.at[1], lam_ref.at[1])
    _lru_scan(af_tm, bf_tm, hf_tm, ab_tm, bb_tm, hb_tm, carry_ref)
    for n in range(LRU_BLOCKS):
        cols = slice(n * LRU_BW, (n + 1) * LRU_BW)
        hf_ref[:, cols] = hf_tm[pl.ds(n, T, stride=LRU_BLOCKS), :].astype(hf_ref.dtype)
        hb_ref[:, cols] = hb_tm[pl.ds(n, T, stride=LRU_BLOCKS), :].astype(hb_ref.dtype)


LRU_HALO = SUBLANES


def _lru(xr_tm, conv_w_tm, conv_b_tm, w_gate, bias_gate, lam):
    S = xr_tm.shape[0] // LRU_BLOCKS
    T = LRU_T
    assert S % T == 0 and T % SCAN_UNROLL == 0 and T % LRU_HALO == 0
    assert CONV_LEFT <= LRU_HALO and CONV_W - 1 - CONV_LEFT <= LRU_HALO
    time_major = pltpu.VMEM((T * LRU_BLOCKS, LRU_BW), F32)
    n_chunks = S // T
    per = T // LRU_HALO
    last_blk = S // LRU_HALO - 1
    W = LRU_WIDTH
    chunk = (T * LRU_BLOCKS, LRU_BW)
    halo = (LRU_HALO * LRU_BLOCKS, LRU_BW)
    const2 = lambda c: (0, 0)
    est = (2 * 2 * (T * W * 4 + 2 * SUBLANES * W * 4) + 2 * 2 * T * W * 4
           + 2 * 2 * LRU_BLOCKS * LRU_BW * 2 * LRU_BW * 2 + (8 * T + 64) * W * 4 + 6 * T * W * 4)
    return pl.pallas_call(
        _lru_kernel,
        grid=(n_chunks,),
        in_specs=[
            pl.BlockSpec(halo, lambda c: (jnp.maximum(c * per - 1, 0), 0)),
            pl.BlockSpec(chunk, lambda c: (c, 0)),
            pl.BlockSpec(halo, lambda c: (jnp.minimum((c + 1) * per, last_blk), 0)),
            pl.BlockSpec(halo, lambda c: (jnp.maximum((n_chunks - 1 - c) * per - 1, 0), 0)),
            pl.BlockSpec(chunk, lambda c: (n_chunks - 1 - c, 0)),
            pl.BlockSpec(halo, lambda c: (jnp.minimum((n_chunks - c) * per, last_blk), 0)),
            pl.BlockSpec((CONV_W, LRU_BLOCKS, LRU_BW), lambda c: (0, 0, 0)),
            pl.BlockSpec((LRU_BLOCKS, LRU_BW), const2),
            pl.BlockSpec((2, LRU_BLOCKS, LRU_BW, 2 * LRU_BW), lambda c: (0, 0, 0, 0)),
            pl.BlockSpec((2, 2, W), lambda c: (0, 0, 0)),
            pl.BlockSpec((2, 1, W), lambda c: (0, 0, 0)),
        ],
        out_specs=[
            pl.BlockSpec((T, W), lambda c: (c, 0)),
            pl.BlockSpec((T, W), lambda c: (n_chunks - 1 - c, 0)),
        ],
        out_shape=[jax.ShapeDtypeStruct((S, W), BF16), jax.ShapeDtypeStruct((S, W), BF16)],
        scratch_shapes=[
            pltpu.VMEM(((T + 2 * LRU_HALO) * LRU_BLOCKS, LRU_BW), F32),
            time_major,
            time_major, time_major, time_major,
            time_major, time_major, time_major,
            pltpu.VMEM((2, SUBLANES, LRU_BW), F32),
        ],
        compiler_params=pltpu.CompilerParams(
            dimension_semantics=("arbitrary",), vmem_limit_bytes=_vmem_limit(est)),
        name="rg_lru",
    )(xr_tm, xr_tm, xr_tm, xr_tm, xr_tm, xr_tm, conv_w_tm, conv_b_tm, w_gate, bias_gate, lam)


OUT_TM = 512


def _out_proj_kernel(ao_ref, ga_ref, hf_ref, hb_ref, gl_ref, x_ref, anw_ref, lnw_ref, w_ref,
                     o_ref):
    def inv_rms(v):
        return lax.rsqrt(jnp.mean(v * v, axis=-1, keepdims=True) + EPS)

    ao = ao_ref[...].astype(F32)
    hl = hf_ref[...].astype(F32) + hb_ref[...].astype(F32)
    ya = (ao * anw_ref[...] * ga_ref[...].astype(F32)).astype(BF16)
    yl = (hl * lnw_ref[...] * gl_ref[...].astype(F32)).astype(BF16)
    mix_a = jnp.dot(ya, w_ref[0:ATTN_WIDTH, :], preferred_element_type=F32) * inv_rms(ao)
    mix_l = jnp.dot(yl, w_ref[ATTN_WIDTH:MIX_WIDTH, :], preferred_element_type=F32) * inv_rms(hl)
    o_ref[...] = x_ref[...] + mix_a + mix_l


def _out_proj(attn_out, g_attn, h_fwd, h_bwd, g_lru, x2, attn_norm_w, lru_norm_w, w_out_bf16):
    S = x2.shape[0]
    tm = OUT_TM
    row = lambda i: (i, 0)
    const = lambda i: (0, 0)
    est = (2 * 5 * tm * ATTN_WIDTH * 2 + 2 * 2 * tm * D_MODEL * 4 + MIX_WIDTH * D_MODEL * 2
           + 4 * tm * D_MODEL * 4)
    return pl.pallas_call(
        _out_proj_kernel,
        grid=(S // tm,),
        in_specs=[
            pl.BlockSpec((tm, ATTN_WIDTH), row),
            pl.BlockSpec((tm, ATTN_WIDTH), row),
            pl.BlockSpec((tm, LRU_WIDTH), row),
            pl.BlockSpec((tm, LRU_WIDTH), row),
            pl.BlockSpec((tm, LRU_WIDTH), row),
            pl.BlockSpec((tm, D_MODEL), row),
            pl.BlockSpec((1, ATTN_WIDTH), const),
            pl.BlockSpec((1, LRU_WIDTH), const),
            pl.BlockSpec((MIX_WIDTH, D_MODEL), const, pipeline_mode=pl.Buffered(1)),
        ],
        out_specs=pl.BlockSpec((tm, D_MODEL), row),
        out_shape=jax.ShapeDtypeStruct((S, D_MODEL), F32),
        compiler_params=pltpu.CompilerParams(
            dimension_semantics=("arbitrary",), vmem_limit_bytes=_vmem_limit(est)),
        name="out_proj",
    )(attn_out, g_attn, h_fwd, h_bwd, g_lru, x2, attn_norm_w, lru_norm_w, w_out_bf16)


def _rope_tables(seq_len, tile_rows):
    rows = seq_len // GRID_W
    inv_freq = ROPE_THETA ** (-jnp.arange(ROPE_PAIRS, dtype=F32) / ROPE_PAIRS)
    ang_r = jnp.arange(rows, dtype=F32)[:, None] * inv_freq[None, :]
    ang_c = jnp.arange(GRID_W, dtype=F32)[:, None] * inv_freq[None, :]
    cr, sr, cc, sc = jnp.cos(ang_r), jnp.sin(ang_r), jnp.cos(ang_c), jnp.sin(ang_c)
    zr, zc = jnp.zeros_like(cr), jnp.zeros_like(cc)
    reps = tile_rows // GRID_W
    return (jnp.concatenate([cr, cr, zr, zr], axis=-1),
            jnp.concatenate([-sr, sr, zr, zr], axis=-1),
            jnp.tile(jnp.concatenate([zc, zc, cc, cc], axis=-1), (reps, 1)),
            jnp.tile(jnp.concatenate([zc, zc, -sc, sc], axis=-1), (reps, 1)))


def _layer(x2, norm_w, w_in, q_norm_w, k_norm_w, conv_w, conv_b, lru_wa, lru_ba, lru_wx, lru_bx,
           lru_lambda, attn_norm_w, lru_norm_w, w_out):
    S = x2.shape[0]
    q_t, k, v_t, k_norm2, g_attn, xr, g_lru = _in_proj(
        x2, norm_w.reshape(1, D_MODEL), w_in.astype(BF16), q_norm_w.reshape(1, HEAD_DIM),
        k_norm_w.reshape(1, HEAD_DIM), *_rope_tables(S, PROJ_TM))
    attn_out = _attention(k_norm2, q_t, k, v_t)
    w_gate = (0.5 * jnp.concatenate([lru_wa, lru_wx], axis=-1)).astype(BF16)
    bias_gate = 0.5 * jnp.stack([lru_ba, lru_bx], axis=1)
    h_fwd, h_bwd = _lru(xr, conv_w.reshape(CONV_W, LRU_BLOCKS, LRU_BW),
                        conv_b.reshape(LRU_BLOCKS, LRU_BW),
                        w_gate, bias_gate, lru_lambda.reshape(2, 1, LRU_WIDTH))
    return _out_proj(attn_out, g_attn, h_fwd, h_bwd, g_lru, x2,
                     attn_norm_w.reshape(1, ATTN_WIDTH), lru_norm_w.reshape(1, LRU_WIDTH),
                     w_out.astype(BF16))


def kernel(x, norm_w, w_in, q_norm_w, k_norm_w, conv_w, conv_b, lru_wa, lru_ba, lru_wx, lru_bx,
           lru_lambda, attn_norm_w, lru_norm_w, w_out):
    B, S, D = x.shape
    assert D == D_MODEL and S % GRID_W == 0
    outs = [_layer(x[b], norm_w, w_in, q_norm_w, k_norm_w, conv_w, conv_b, lru_wa, lru_ba,
                   lru_wx, lru_bx, lru_lambda, attn_norm_w, lru_norm_w, w_out)
            for b in range(B)]
    return jnp.stack(outs, axis=0) if B > 1 else outs[0][None]
```

```python
import functools
import math

import jax
import jax.numpy as jnp
from jax import lax
from jax.experimental import pallas as pl
from jax.experimental.pallas import tpu as pltpu

F32 = jnp.float32
BF16 = jnp.bfloat16

D_MODEL = 2048
N_HEADS = 8
N_KV_HEADS = 2
GROUP = N_HEADS // N_KV_HEADS
HEAD_DIM = 128
ATTN_WIDTH = N_HEADS * HEAD_DIM
KV_WIDTH = N_KV_HEADS * HEAD_DIM
ROPE_THETA = 10000.0
ROPE_PAIRS = HEAD_DIM // 4
GRID_W = 64
LRU_WIDTH = D_MODEL // 2
LRU_BLOCKS = 8
LRU_BW = LRU_WIDTH // LRU_BLOCKS
LRU_C = 8.0
CONV_W = 4
CONV_LEFT = 2
MIX_WIDTH = ATTN_WIDTH + LRU_WIDTH
IN_WIDTH = 2 * ATTN_WIDTH + 2 * KV_WIDTH + 2 * LRU_WIDTH
EPS = 1e-6

COL_Q = 0
COL_K = ATTN_WIDTH
COL_V = ATTN_WIDTH + KV_WIDTH
COL_GA = ATTN_WIDTH + 2 * KV_WIDTH
COL_XR = COL_GA + ATTN_WIDTH
COL_GL = COL_XR + LRU_WIDTH

SUBLANES = 8
LANES = 128
VMEM_LIMIT_CAP = 60000 * 1024
MIB = 1024 * 1024

NEG_BIG = -1e30
TINY = 1e-30
LOG2_E = math.log2(math.e)


def _vmem_limit(estimate_bytes):
    return int(min(VMEM_LIMIT_CAP, estimate_bytes + 8 * MIB))


def _sigmoid(x):
    return 0.5 * jnp.tanh(0.5 * x) + 0.5


PROJ_TM = 256
PROJ_TN = 512
KN_ROWS = SUBLANES // N_KV_HEADS


def _in_proj_kernel(x_ref, nw_ref, w_ref, qw_ref, kw_ref, rcos_ref, rsin_ref, ccos_ref, csin_ref,
                    q_ref, k_ref, v_ref, kn_ref, ga_ref, xr_ref, gl_ref):
    @pl.when(pl.program_id(0) == 0)
    def _():
        kn_ref[...] = jnp.zeros(kn_ref.shape, F32)

    x = x_ref[...]
    ms = jnp.mean(x * x, axis=-1, keepdims=True)
    inv_rms = lax.rsqrt(ms + EPS)
    h = (x * nw_ref[...]).astype(BF16)

    def rope_table(row_ref, col_ref):
        grid_rows = x_ref.shape[0] // GRID_W
        first = pl.program_id(0) * grid_rows
        lines = [jnp.broadcast_to(row_ref[pl.ds(first + r, 1), :], (GRID_W, HEAD_DIM))
                 for r in range(grid_rows)]
        return jnp.concatenate(lines, axis=0) + col_ref[...]

    cos = rope_table(rcos_ref, ccos_ref)
    sin = rope_table(rsin_ref, csin_ref)
    lane = lax.broadcasted_iota(jnp.int32, (1, HEAD_DIM), 1)
    low_half = (lane % (2 * ROPE_PAIRS)) < ROPE_PAIRS

    def proj(col, width):
        return jnp.dot(h, w_ref[:, col:col + width], preferred_element_type=F32) * inv_rms

    def norm_rope(seg, gain, scale):
        ms_h = jnp.mean(seg * seg, axis=-1, keepdims=True)
        y = seg * lax.rsqrt(ms_h + EPS) * gain
        partner = jnp.where(low_half, pltpu.roll(y, HEAD_DIM - ROPE_PAIRS, 1),
                            pltpu.roll(y, ROPE_PAIRS, 1))
        return (y * cos + partner * sin) * scale

    scale = HEAD_DIM ** -0.5 * LOG2_E
    qw = qw_ref[...]
    kw = kw_ref[...]
    tm = x_ref.shape[0]
    for cb in range(ATTN_WIDTH // PROJ_TN):
        acc = proj(COL_Q + cb * PROJ_TN, PROJ_TN)
        for hh in range(PROJ_TN // HEAD_DIM):
            head = cb * (PROJ_TN // HEAD_DIM) + hh
            seg = acc[:, hh * HEAD_DIM:(hh + 1) * HEAD_DIM]
            qt = norm_rope(seg, qw, scale).T.astype(BF16)
            lane0 = (head % GROUP) * tm
            q_ref[head // GROUP, 0, :, lane0:lane0 + tm] = qt
    acc = proj(COL_K, 2 * KV_WIDTH)
    for hh in range(N_KV_HEADS):
        kb = norm_rope(acc[:, hh * HEAD_DIM:(hh + 1) * HEAD_DIM], kw, 1.0).astype(BF16)
        k_ref[hh] = kb
        kf = kb.astype(F32)
        n2 = jnp.max(jnp.sum(kf * kf, axis=-1, keepdims=True), axis=0, keepdims=True)
        rows = slice(hh * KN_ROWS, (hh + 1) * KN_ROWS)
        kn_ref[rows, :] = jnp.maximum(kn_ref[rows, :], n2)
        vseg = acc[:, KV_WIDTH + hh * HEAD_DIM:KV_WIDTH + (hh + 1) * HEAD_DIM]
        v_ref[hh] = vseg.T.astype(BF16)
    for cb in range(ATTN_WIDTH // PROJ_TN):
        g = proj(COL_GA + cb * PROJ_TN, PROJ_TN)
        ga_ref[:, cb * PROJ_TN:(cb + 1) * PROJ_TN] = (g * _sigmoid(g)).astype(BF16)
    for cb in range(LRU_WIDTH // PROJ_TN):
        acc = proj(COL_XR + cb * PROJ_TN, PROJ_TN)
        for q in range(PROJ_TN // LANES):
            n = cb * (PROJ_TN // LANES) + q
            xr_ref[pl.ds(n, tm, stride=LRU_BLOCKS), :] = acc[:, q * LANES:(q + 1) * LANES]
    for cb in range(LRU_WIDTH // PROJ_TN):
        g = proj(COL_GL + cb * PROJ_TN, PROJ_TN)
        gl_ref[:, cb * PROJ_TN:(cb + 1) * PROJ_TN] = (g * _sigmoid(g)).astype(BF16)


def _in_proj(x2, norm_w, w_in_bf16, q_norm_w, k_norm_w, row_cos, row_sin, col_cos, col_sin):
    S = x2.shape[0]
    tm = PROJ_TM
    assert tm % GRID_W == 0 and S % tm == 0
    const = lambda i: (0, 0)
    est = (2 * tm * D_MODEL * 4 + D_MODEL * IN_WIDTH * 2 + 4 * tm * HEAD_DIM * 4
           + 2 * (3 * tm * ATTN_WIDTH * 2 + 2 * tm * KV_WIDTH * 2 + tm * LRU_WIDTH * 4)
           + tm * D_MODEL * 6 + 4 * tm * PROJ_TN * 4)
    return pl.pallas_call(
        _in_proj_kernel,
        grid=(S // tm,),
        in_specs=[
            pl.BlockSpec((tm, D_MODEL), lambda i: (i, 0)),
            pl.BlockSpec((1, D_MODEL), const),
            pl.BlockSpec((D_MODEL, IN_WIDTH), const, pipeline_mode=pl.Buffered(1)),
            pl.BlockSpec((1, HEAD_DIM), const),
            pl.BlockSpec((1, HEAD_DIM), const),
            pl.BlockSpec((S // GRID_W, HEAD_DIM), const),
            pl.BlockSpec((S // GRID_W, HEAD_DIM), const),
            pl.BlockSpec((tm, HEAD_DIM), const),
            pl.BlockSpec((tm, HEAD_DIM), const),
        ],
        out_specs=[
            pl.BlockSpec((N_KV_HEADS, 1, HEAD_DIM, GROUP * tm), lambda i: (0, i, 0, 0)),
            pl.BlockSpec((N_KV_HEADS, tm, HEAD_DIM), lambda i: (0, i, 0)),
            pl.BlockSpec((N_KV_HEADS, HEAD_DIM, tm), lambda i: (0, 0, i)),
            pl.BlockSpec((N_KV_HEADS * KN_ROWS, LANES), const),
            pl.BlockSpec((tm, ATTN_WIDTH), lambda i: (i, 0)),
            pl.BlockSpec((tm * LRU_BLOCKS, LANES), lambda i: (i, 0)),
            pl.BlockSpec((tm, LRU_WIDTH), lambda i: (i, 0)),
        ],
        out_shape=[
            jax.ShapeDtypeStruct((N_KV_HEADS, S // tm, HEAD_DIM, GROUP * tm), BF16),
            jax.ShapeDtypeStruct((N_KV_HEADS, S, HEAD_DIM), BF16),
            jax.ShapeDtypeStruct((N_KV_HEADS, HEAD_DIM, S), BF16),
            jax.ShapeDtypeStruct((N_KV_HEADS * KN_ROWS, LANES), F32),
            jax.ShapeDtypeStruct((S, ATTN_WIDTH), BF16),
            jax.ShapeDtypeStruct((S * LRU_BLOCKS, LANES), F32),
            jax.ShapeDtypeStruct((S, LRU_WIDTH), BF16),
        ],
        compiler_params=pltpu.CompilerParams(
            dimension_semantics=("arbitrary",), vmem_limit_bytes=_vmem_limit(est)),
        name="in_proj",
    )(x2, norm_w, w_in_bf16, q_norm_w, k_norm_w, row_cos, row_sin, col_cos, col_sin)


ATTN_Q_TILES = 2
ATTN_TK = 1024


SOFTMAX_ROWS = 32


def _sublane_allmax(x):
    for shift in (4, 2, 1):
        x = jnp.maximum(x, pltpu.roll(x, shift, 0))
    return x


SAFE_LOGIT_BOUND = 55.0
BOUND_SLACK = 1.0 + 2.0 ** -10
LANE_BLOCK = 256


def _attn_kernel(kn_ref, q_tiles_ref, k_ref, vt_ref, o_ref, qt_ref, m_ref, l_ref, acc_ref, s_buf,
                 tmax_buf, p_buf, alpha_buf, *, tk):
    n_tiles, _, tile_lanes = q_tiles_ref.shape
    tile_rows = tile_lanes // GROUP
    lanes = n_tiles * tile_lanes
    n_kv = k_ref.shape[0] // tk
    n_groups = tk // SUBLANES
    chunk = SOFTMAX_ROWS // SUBLANES
    acc_shape = (HEAD_DIM // SUBLANES, SUBLANES, lanes)

    for t in range(n_tiles):
        qt_ref[:, t * tile_lanes:(t + 1) * tile_lanes] = q_tiles_ref[t]
    qf = qt_ref[...].astype(F32)
    qn2 = jnp.sum(qf * qf, axis=0, keepdims=True)
    kn = kn_ref[...]
    kn2 = jnp.where(pl.program_id(0) == 0, kn[0:1, :], kn[KN_ROWS:KN_ROWS + 1, :])
    bound = jnp.sqrt(qn2 * jnp.tile(kn2, (1, lanes // LANES))) * BOUND_SLACK
    single_pass = jnp.max(bound) <= SAFE_LOGIT_BOUND

    def kv_rows(j):
        return pl.ds(pl.multiple_of(j * tk, tk), tk)

    def finish():
        l_tot = jnp.sum(l_ref[...], axis=0, keepdims=True)
        out_t = acc_ref[...].reshape(HEAD_DIM, lanes) * (1.0 / l_tot)
        for t in range(n_tiles):
            for h in range(GROUP):
                lane0 = t * tile_lanes + h * tile_rows
                o_ref[t * tile_rows:(t + 1) * tile_rows, h * HEAD_DIM:(h + 1) * HEAD_DIM] = (
                    out_t[:, lane0:lane0 + tile_rows].T.astype(o_ref.dtype))

    l_ref[...] = jnp.zeros((SUBLANES, lanes), F32)
    acc_ref[...] = jnp.zeros(acc_shape, F32)

    @pl.when(single_pass)
    def _():
        m_ref[...] = jnp.broadcast_to(bound, (SUBLANES, lanes))

        def probs(j, slot):
            kt = k_ref[kv_rows(j), :]
            for lb in range(lanes // LANE_BLOCK):
                cols = slice(lb * LANE_BLOCK, (lb + 1) * LANE_BLOCK)
                s = jnp.dot(kt, qt_ref[:, cols], preferred_element_type=F32)
                p = jnp.exp2(s.reshape(n_groups, SUBLANES, LANE_BLOCK) - m_ref[:, cols])
                l_ref[:, cols] += jnp.sum(p, axis=0)
                p_buf[slot, :, cols] = p.reshape(tk, LANE_BLOCK).astype(BF16)

        def values(j, slot):
            pv = jnp.dot(vt_ref[:, kv_rows(j)], p_buf[slot], preferred_element_type=F32)
            acc_ref[...] += pv.reshape(acc_shape)

        probs(0, 0)

        def body(t, _):
            j = 2 * t
            probs(j + 1, 1)
            values(j, 0)
            probs(j + 2, 0)
            values(j + 1, 1)
            return 0

        lax.fori_loop(0, n_kv // 2 - 1, body, 0)
        probs(n_kv - 1, 1)
        values(n_kv - 2, 0)
        values(n_kv - 1, 1)
        finish()

    @pl.when(jnp.logical_not(single_pass))
    def _():
        def scores(j, slot):
            s = jnp.dot(k_ref[kv_rows(j), :], qt_ref[...], preferred_element_type=F32)
            s = s.reshape(n_groups, SUBLANES, lanes)
            s_buf[slot] = s
            tmax_buf[slot] = jnp.max(s, axis=0)

        def softmax(slot):
            m_old = m_ref[...]
            m_new = jnp.maximum(m_old, _sublane_allmax(tmax_buf[slot]))
            alpha = jnp.exp2(m_old - m_new)
            psum = jnp.zeros((SUBLANES, lanes), F32)
            for c in range(n_groups // chunk):
                s = s_buf[slot, c * chunk:(c + 1) * chunk]
                p = jnp.exp2(s - m_new)
                psum = psum + jnp.sum(p, axis=0)
                p_buf[slot, c * SOFTMAX_ROWS:(c + 1) * SOFTMAX_ROWS, :] = (
                    p.reshape(SOFTMAX_ROWS, lanes).astype(BF16))
            l_ref[...] = alpha * l_ref[...] + psum
            m_ref[...] = m_new
            alpha_buf[slot] = alpha

        def values(j, slot):
            pv = jnp.dot(vt_ref[:, kv_rows(j)], p_buf[slot], preferred_element_type=F32)
            acc_ref[...] = alpha_buf[slot] * acc_ref[...] + pv.reshape(acc_shape)

        m_ref[...] = jnp.full((SUBLANES, lanes), NEG_BIG, F32)
        p_buf[1] = jnp.zeros((tk, lanes), BF16)
        alpha_buf[1] = jnp.ones((SUBLANES, lanes), F32)
        scores(0, 0)

        def body(t, _):
            for slot in range(2):
                cur = 2 * t + slot
                nxt = jnp.where(cur + 1 < n_kv, cur + 1, 0)
                prv = jnp.maximum(cur - 1, 0)
                scores(nxt, 1 - slot)
                softmax(slot)
                values(prv, 1 - slot)
            return 0

        lax.fori_loop(0, n_kv // 2, body, 0)
        values(n_kv - 1, 1)
        finish()


def _attention(k_norm2, q_t, k, v_t):
    S = k.shape[1]
    tk = ATTN_TK
    n_tiles = ATTN_Q_TILES
    tile_lanes = q_t.shape[-1]
    tq = n_tiles * (tile_lanes // GROUP)
    assert S % (2 * tk) == 0 and S >= 4 * tk and tk % SOFTMAX_ROWS == 0 and S % tq == 0
    lanes = n_tiles * tile_lanes
    est = (3 * HEAD_DIM * lanes * 2 + 2 * 2 * S * HEAD_DIM * 2 + 2 * tq * GROUP * HEAD_DIM * 2
           + (HEAD_DIM + 6 * SUBLANES) * lanes * 4 + 2 * tk * lanes * 6 + 2 * tk * lanes * 4)
    return pl.pallas_call(
        functools.partial(_attn_kernel, tk=tk),
        grid=(N_KV_HEADS, S // tq),
        in_specs=[
            pl.BlockSpec((N_KV_HEADS * KN_ROWS, LANES), lambda g, i: (0, 0)),
            pl.BlockSpec((None, n_tiles, HEAD_DIM, tile_lanes), lambda g, i: (g, i, 0, 0)),
            pl.BlockSpec((None, S, HEAD_DIM), lambda g, i: (g, 0, 0)),
            pl.BlockSpec((None, HEAD_DIM, S), lambda g, i: (g, 0, 0)),
        ],
        out_specs=pl.BlockSpec((tq, GROUP * HEAD_DIM), lambda g, i: (i, g)),
        out_shape=jax.ShapeDtypeStruct((S, ATTN_WIDTH), BF16),
        scratch_shapes=[
            pltpu.VMEM((HEAD_DIM, lanes), BF16),
            pltpu.VMEM((SUBLANES, lanes), F32),
            pltpu.VMEM((SUBLANES, lanes), F32),
            pltpu.VMEM((HEAD_DIM // SUBLANES, SUBLANES, lanes), F32),
            pltpu.VMEM((2, tk // SUBLANES, SUBLANES, lanes), F32),
            pltpu.VMEM((2, SUBLANES, lanes), F32),
            pltpu.VMEM((2, tk, lanes), BF16),
            pltpu.VMEM((2, SUBLANES, lanes), F32),
        ],
        compiler_params=pltpu.CompilerParams(
            dimension_semantics=("arbitrary", "arbitrary"), vmem_limit_bytes=_vmem_limit(est)),
        name="attention",
    )(k_norm2, q_t, k, v_t)


LRU_T = 256
SCAN_UNROLL = 16
assert LRU_WIDTH == SUBLANES * LANES and LRU_BLOCKS == SUBLANES


def _lru_gates(prev_ref, cur_ref, next_ref, ext_tm, xc_tm, a_tm, b_tm, has_prev, has_next,
               cw_ref, cb_ref, w_ref, bias_ref, lam_ref):
    rows = cur_ref.shape[0]
    T = rows // LRU_BLOCKS
    halo = prev_ref.shape[0]
    ext_tm[0:halo, :] = jnp.where(has_prev, prev_ref[...], 0.0)
    ext_tm[halo:halo + rows, :] = cur_ref[...]
    ext_tm[halo + rows:2 * halo + rows, :] = jnp.where(has_next, next_ref[...], 0.0)
    base = halo - CONV_LEFT * LRU_BLOCKS
    xh = cb_ref[...]
    for j in range(CONV_W):
        start = base + j * LRU_BLOCKS
        tap = ext_tm[start:start + rows, :].reshape(T, LRU_BLOCKS, LRU_BW)
        xh = xh + cw_ref[j] * tap
    xc_tm[...] = xh.reshape(rows, LRU_BW)
    lam = lam_ref[...]
    neg = -lam
    softplus = jnp.maximum(neg, 0.0) + jnp.log1p(jnp.exp(-jnp.abs(neg)))
    half_rate = (-0.5 * LRU_C * LOG2_E) * softplus
    for n in range(LRU_BLOCKS):
        cols = slice(n * LRU_BW, (n + 1) * LRU_BW)
        block_n = pl.ds(n, T, stride=LRU_BLOCKS)
        xb = xc_tm[block_n, :]
        z = jnp.dot(xb.astype(BF16), w_ref[n], preferred_element_type=F32)
        tanh_r = jnp.tanh(z[:, :LRU_BW] + bias_ref[0:1, cols])
        tanh_i = jnp.tanh(z[:, LRU_BW:] + bias_ref[1:2, cols])
        rate = half_rate[:, cols]
        a = jnp.exp2(tanh_r * rate + rate)
        one_minus_a2 = 1.0 - a * a
        mult = one_minus_a2 * lax.rsqrt(jnp.maximum(one_minus_a2, TINY))
        half_in = mult * xb
        a_tm[block_n, :] = a
        b_tm[block_n, :] = half_in * tanh_i + half_in


def _lru_scan(af_tm, bf_tm, ab_tm, bb_tm, hf_ref, hb_ref, carry_ref):
    T = af_tm.shape[0] // LRU_BLOCKS
    U = SCAN_UNROLL

    def store(out_ref, steps, t0):
        block = pltpu.einshape("tnc->ntc", jnp.stack(steps))
        rows = pl.ds(pl.multiple_of(t0, U), U)
        for n in range(LRU_BLOCKS):
            out_ref[rows, n * LRU_BW:(n + 1) * LRU_BW] = block[n].astype(out_ref.dtype)

    def body(i, carry):
        h_f, h_b = carry
        fwd, bwd = [], []
        for u in range(U):
            t = i * U + u
            rf = pl.ds(pl.multiple_of(t * LRU_BLOCKS, LRU_BLOCKS), LRU_BLOCKS)
            rb = pl.ds(pl.multiple_of((T - 1 - t) * LRU_BLOCKS, LRU_BLOCKS), LRU_BLOCKS)
            h_f = af_tm[rf, :] * h_f + bf_tm[rf, :]
            h_b = ab_tm[rb, :] * h_b + bb_tm[rb, :]
            fwd.append(h_f)
            bwd.append(h_b)
        store(hf_ref, fwd, i * U)
        store(hb_ref, bwd[::-1], T - U - i * U)
        return h_f, h_b

    h_f, h_b = lax.fori_loop(0, T // U, body, (carry_ref[0], carry_ref[1]))
    carry_ref[0] = h_f
    carry_ref[1] = h_b


def _lru_kernel(fp_ref, fc_ref, fn_ref, bp_ref, bc_ref, bn_ref, cw_ref, cb_ref, w_ref, bias_ref,
                lam_ref, hf_ref, hb_ref, ext_tm, xc_tm, af_tm, bf_tm, ab_tm, bb_tm, carry_ref):
    c = pl.program_id(0)
    n_chunks = pl.num_programs(0)

    @pl.when(c == 0)
    def _():
        carry_ref[...] = jnp.zeros(carry_ref.shape, F32)

    _lru_gates(fp_ref, fc_ref, fn_ref, ext_tm, xc_tm, af_tm, bf_tm, c > 0, c < n_chunks - 1,
               cw_ref, cb_ref, w_ref.at[0], bias_ref.at[0], lam_ref.at[0])
    _lru_gates(bp_ref, bc_ref, bn_ref, ext_tm, xc_tm, ab_tm, bb_tm, c < n_chunks - 1, c > 0,
               cw_ref, cb_ref, w_ref.at[1], bias_ref.at[1], lam_ref.at[1])
    _lru_scan(af_tm, bf_tm, ab_tm, bb_tm, hf_ref, hb_ref, carry_ref)


LRU_HALO = SUBLANES


def _lru(xr_tm, conv_w_tm, conv_b_tm, w_gate, bias_gate, lam):
    S = xr_tm.shape[0] // LRU_BLOCKS
    T = LRU_T
    assert S % T == 0 and T % SCAN_UNROLL == 0 and T % LRU_HALO == 0
    assert CONV_LEFT <= LRU_HALO and CONV_W - 1 - CONV_LEFT <= LRU_HALO
    time_major = pltpu.VMEM((T * LRU_BLOCKS, LRU_BW), F32)
    n_chunks = S // T
    per = T // LRU_HALO
    last_blk = S // LRU_HALO - 1
    W = LRU_WIDTH
    chunk = (T * LRU_BLOCKS, LRU_BW)
    halo = (LRU_HALO * LRU_BLOCKS, LRU_BW)
    const2 = lambda c: (0, 0)
    est = (2 * 2 * (T * W * 4 + 2 * SUBLANES * W * 4) + 2 * 2 * T * W * 4
           + 2 * 2 * LRU_BLOCKS * LRU_BW * 2 * LRU_BW * 2 + (6 * T + 64) * W * 4 + 6 * T * W * 4)
    return pl.pallas_call(
        _lru_kernel,
        grid=(n_chunks,),
        in_specs=[
            pl.BlockSpec(halo, lambda c: (jnp.maximum(c * per - 1, 0), 0)),
            pl.BlockSpec(chunk, lambda c: (c, 0)),
            pl.BlockSpec(halo, lambda c: (jnp.minimum((c + 1) * per, last_blk), 0)),
            pl.BlockSpec(halo, lambda c: (jnp.maximum((n_chunks - 1 - c) * per - 1, 0), 0)),
            pl.BlockSpec(chunk, lambda c: (n_chunks - 1 - c, 0)),
            pl.BlockSpec(halo, lambda c: (jnp.minimum((n_chunks - c) * per, last_blk), 0)),
            pl.BlockSpec((CONV_W, LRU_BLOCKS, LRU_BW), lambda c: (0, 0, 0)),
            pl.BlockSpec((LRU_BLOCKS, LRU_BW), const2),
            pl.BlockSpec((2, LRU_BLOCKS, LRU_BW, 2 * LRU_BW), lambda c: (0, 0, 0, 0)),
            pl.BlockSpec((2, 2, W), lambda c: (0, 0, 0)),
            pl.BlockSpec((2, 1, W), lambda c: (0, 0, 0)),
        ],
        out_specs=[
            pl.BlockSpec((T, W), lambda c: (c, 0)),
            pl.BlockSpec((T, W), lambda c: (n_chunks - 1 - c, 0)),
        ],
        out_shape=[jax.ShapeDtypeStruct((S, W), BF16), jax.ShapeDtypeStruct((S, W), BF16)],
        scratch_shapes=[
            pltpu.VMEM(((T + 2 * LRU_HALO) * LRU_BLOCKS, LRU_BW), F32),
            time_major,
            time_major, time_major,
            time_major, time_major,
            pltpu.VMEM((2, SUBLANES, LRU_BW), F32),
        ],
        compiler_params=pltpu.CompilerParams(
            dimension_semantics=("arbitrary",), vmem_limit_bytes=_vmem_limit(est)),
        name="rg_lru",
    )(xr_tm, xr_tm, xr_tm, xr_tm, xr_tm, xr_tm, conv_w_tm, conv_b_tm, w_gate, bias_gate, lam)


OUT_TM = 512


def _out_proj_kernel(ao_ref, ga_ref, hf_ref, hb_ref, gl_ref, x_ref, anw_ref, lnw_ref, w_ref,
                     o_ref):
    def inv_rms(v):
        return lax.rsqrt(jnp.mean(v * v, axis=-1, keepdims=True) + EPS)

    ao = ao_ref[...].astype(F32)
    hl = hf_ref[...].astype(F32) + hb_ref[...].astype(F32)
    ya = (ao * anw_ref[...] * ga_ref[...].astype(F32)).astype(BF16)
    yl = (hl * lnw_ref[...] * gl_ref[...].astype(F32)).astype(BF16)
    mix_a = jnp.dot(ya, w_ref[0:ATTN_WIDTH, :], preferred_element_type=F32) * inv_rms(ao)
    mix_l = jnp.dot(yl, w_ref[ATTN_WIDTH:MIX_WIDTH, :], preferred_element_type=F32) * inv_rms(hl)
    o_ref[...] = x_ref[...] + mix_a + mix_l


def _out_proj(attn_out, g_attn, h_fwd, h_bwd, g_lru, x2, attn_norm_w, lru_norm_w, w_out_bf16):
    S = x2.shape[0]
    tm = OUT_TM
    row = lambda i: (i, 0)
    const = lambda i: (0, 0)
    est = (2 * 5 * tm * ATTN_WIDTH * 2 + 2 * 2 * tm * D_MODEL * 4 + MIX_WIDTH * D_MODEL * 2
           + 4 * tm * D_MODEL * 4)
    return pl.pallas_call(
        _out_proj_kernel,
        grid=(S // tm,),
        in_specs=[
            pl.BlockSpec((tm, ATTN_WIDTH), row),
            pl.BlockSpec((tm, ATTN_WIDTH), row),
            pl.BlockSpec((tm, LRU_WIDTH), row),
            pl.BlockSpec((tm, LRU_WIDTH), row),
            pl.BlockSpec((tm, LRU_WIDTH), row),
            pl.BlockSpec((tm, D_MODEL), row),
            pl.BlockSpec((1, ATTN_WIDTH), const),
            pl.BlockSpec((1, LRU_WIDTH), const),
            pl.BlockSpec((MIX_WIDTH, D_MODEL), const, pipeline_mode=pl.Buffered(1)),
        ],
        out_specs=pl.BlockSpec((tm, D_MODEL), row),
        out_shape=jax.ShapeDtypeStruct((S, D_MODEL), F32),
        compiler_params=pltpu.CompilerParams(
            dimension_semantics=("arbitrary",), vmem_limit_bytes=_vmem_limit(est)),
        name="out_proj",
    )(attn_out, g_attn, h_fwd, h_bwd, g_lru, x2, attn_norm_w, lru_norm_w, w_out_bf16)


def _rope_tables(seq_len, tile_rows):
    rows = seq_len // GRID_W
    inv_freq = ROPE_THETA ** (-jnp.arange(ROPE_PAIRS, dtype=F32) / ROPE_PAIRS)
    ang_r = jnp.arange(rows, dtype=F32)[:, None] * inv_freq[None, :]
    ang_c = jnp.arange(GRID_W, dtype=F32)[:, None] * inv_freq[None, :]
    cr, sr, cc, sc = jnp.cos(ang_r), jnp.sin(ang_r), jnp.cos(ang_c), jnp.sin(ang_c)
    zr, zc = jnp.zeros_like(cr), jnp.zeros_like(cc)
    reps = tile_rows // GRID_W
    return (jnp.concatenate([cr, cr, zr, zr], axis=-1),
            jnp.concatenate([-sr, sr, zr, zr], axis=-1),
            jnp.tile(jnp.concatenate([zc, zc, cc, cc], axis=-1), (reps, 1)),
            jnp.tile(jnp.concatenate([zc, zc, -sc, sc], axis=-1), (reps, 1)))


def _layer(x2, norm_w, w_in, q_norm_w, k_norm_w, conv_w, conv_b, lru_wa, lru_ba, lru_wx, lru_bx,
           lru_lambda, attn_norm_w, lru_norm_w, w_out):
    S = x2.shape[0]
    q_t, k, v_t, k_norm2, g_attn, xr, g_lru = _in_proj(
        x2, norm_w.reshape(1, D_MODEL), w_in.astype(BF16), q_norm_w.reshape(1, HEAD_DIM),
        k_norm_w.reshape(1, HEAD_DIM), *_rope_tables(S, PROJ_TM))
    attn_out = _attention(k_norm2, q_t, k, v_t)
    w_gate = jnp.concatenate([lru_wa, lru_wx], axis=-1).astype(BF16)
    bias_gate = 0.5 * jnp.stack([lru_ba, lru_bx], axis=1)
    h_fwd, h_bwd = _lru(xr, 0.5 * conv_w.reshape(CONV_W, LRU_BLOCKS, LRU_BW),
                        0.5 * conv_b.reshape(LRU_BLOCKS, LRU_BW),
                        w_gate, bias_gate, lru_lambda.reshape(2, 1, LRU_WIDTH))
    return _out_proj(attn_out, g_attn, h_fwd, h_bwd, g_lru, x2,
                     attn_norm_w.reshape(1, ATTN_WIDTH), lru_norm_w.reshape(1, LRU_WIDTH),
                     w_out.astype(BF16))


def kernel(x, norm_w, w_in, q_norm_w, k_norm_w, conv_w, conv_b, lru_wa, lru_ba, lru_wx, lru_bx,
           lru_lambda, attn_norm_w, lru_norm_w, w_out):
    B, S, D = x.shape
    assert D == D_MODEL and S % GRID_W == 0
    outs = [_layer(x[b], norm_w, w_in, q_norm_w, k_norm_w, conv_w, conv_b, lru_wa, lru_ba,
                   lru_wx, lru_bx, lru_lambda, attn_norm_w, lru_norm_w, w_out)
            for b in range(B)]
    return jnp.stack(outs, axis=0) if B > 1 else outs[0][None]
```

```python
import functools
import math

import jax
import jax.numpy as jnp
from jax import lax
from jax.experimental import pallas as pl
from jax.experimental.pallas import tpu as pltpu

F32 = jnp.float32
BF16 = jnp.bfloat16

D_MODEL = 2048
N_HEADS = 8
N_KV_HEADS = 2
GROUP = N_HEADS // N_KV_HEADS
HEAD_DIM = 128
ATTN_WIDTH = N_HEADS * HEAD_DIM
KV_WIDTH = N_KV_HEADS * HEAD_DIM
ROPE_THETA = 10000.0
ROPE_PAIRS = HEAD_DIM // 4
GRID_W = 64
LRU_WIDTH = D_MODEL // 2
LRU_BLOCKS = 8
LRU_BW = LRU_WIDTH // LRU_BLOCKS
LRU_C = 8.0
CONV_W = 4
CONV_LEFT = 2
MIX_WIDTH = ATTN_WIDTH + LRU_WIDTH
IN_WIDTH = 2 * ATTN_WIDTH + 2 * KV_WIDTH + 2 * LRU_WIDTH
EPS = 1e-6

COL_Q = 0
COL_K = ATTN_WIDTH
COL_V = ATTN_WIDTH + KV_WIDTH
COL_GA = ATTN_WIDTH + 2 * KV_WIDTH
COL_XR = COL_GA + ATTN_WIDTH
COL_GL = COL_XR + LRU_WIDTH

SUBLANES = 8
LANES = 128
VMEM_LIMIT_CAP = 60000 * 1024
MIB = 1024 * 1024

NEG_BIG = -1e30
TINY = 1e-30
LOG2_E = math.log2(math.e)


def _vmem_limit(estimate_bytes):
    return int(min(VMEM_LIMIT_CAP, estimate_bytes + 8 * MIB))


def _sigmoid(x):
    return 0.5 * jnp.tanh(0.5 * x) + 0.5


PROJ_TM = 256
PROJ_TN = 512
KN_ROWS = SUBLANES // N_KV_HEADS


def _in_proj_kernel(x_ref, nw_ref, w_ref, qw_ref, kw_ref, rcos_ref, rsin_ref, ccos_ref, csin_ref,
                    q_ref, k_ref, v_ref, kn_ref, ga_ref, xr_ref, gl_ref):
    @pl.when(pl.program_id(0) == 0)
    def _():
        kn_ref[...] = jnp.zeros(kn_ref.shape, F32)

    x = x_ref[...]
    ms = jnp.mean(x * x, axis=-1, keepdims=True)
    inv_rms = lax.rsqrt(ms + EPS)
    h = (x * nw_ref[...]).astype(BF16)

    def rope_table(row_ref, col_ref):
        grid_rows = x_ref.shape[0] // GRID_W
        first = pl.program_id(0) * grid_rows
        lines = [jnp.broadcast_to(row_ref[pl.ds(first + r, 1), :], (GRID_W, HEAD_DIM))
                 for r in range(grid_rows)]
        return jnp.concatenate(lines, axis=0) + col_ref[...]

    cos = rope_table(rcos_ref, ccos_ref)
    sin = rope_table(rsin_ref, csin_ref)
    lane = lax.broadcasted_iota(jnp.int32, (1, HEAD_DIM), 1)
    low_half = (lane % (2 * ROPE_PAIRS)) < ROPE_PAIRS

    def proj(col, width):
        return jnp.dot(h, w_ref[:, col:col + width], preferred_element_type=F32) * inv_rms

    def norm_rope(seg, gain, scale):
        ms_h = jnp.mean(seg * seg, axis=-1, keepdims=True)
        y = seg * lax.rsqrt(ms_h + EPS) * gain
        partner = jnp.where(low_half, pltpu.roll(y, HEAD_DIM - ROPE_PAIRS, 1),
                            pltpu.roll(y, ROPE_PAIRS, 1))
        return (y * cos + partner * sin) * scale

    scale = HEAD_DIM ** -0.5 * LOG2_E
    qw = qw_ref[...]
    kw = kw_ref[...]
    tm = x_ref.shape[0]
    for cb in range(ATTN_WIDTH // PROJ_TN):
        acc = proj(COL_Q + cb * PROJ_TN, PROJ_TN)
        for hh in range(PROJ_TN // HEAD_DIM):
            head = cb * (PROJ_TN // HEAD_DIM) + hh
            seg = acc[:, hh * HEAD_DIM:(hh + 1) * HEAD_DIM]
            qt = norm_rope(seg, qw, scale).T.astype(BF16)
            lane0 = (head % GROUP) * tm
            q_ref[head // GROUP, 0, :, lane0:lane0 + tm] = qt
    acc = proj(COL_K, 2 * KV_WIDTH)
    for hh in range(N_KV_HEADS):
        kb = norm_rope(acc[:, hh * HEAD_DIM:(hh + 1) * HEAD_DIM], kw, 1.0).astype(BF16)
        k_ref[hh] = kb
        kf = kb.astype(F32)
        n2 = jnp.max(jnp.sum(kf * kf, axis=-1, keepdims=True), axis=0, keepdims=True)
        rows = slice(hh * KN_ROWS, (hh + 1) * KN_ROWS)
        kn_ref[rows, :] = jnp.maximum(kn_ref[rows, :], n2)
        vseg = acc[:, KV_WIDTH + hh * HEAD_DIM:KV_WIDTH + (hh + 1) * HEAD_DIM]
        v_ref[hh] = vseg.T.astype(BF16)
    for cb in range(ATTN_WIDTH // PROJ_TN):
        g = proj(COL_GA + cb * PROJ_TN, PROJ_TN)
        ga_ref[:, cb * PROJ_TN:(cb + 1) * PROJ_TN] = (g * _sigmoid(g)).astype(BF16)
    for cb in range(LRU_WIDTH // PROJ_TN):
        g = proj(COL_GL + cb * PROJ_TN, PROJ_TN)
        gl_ref[:, cb * PROJ_TN:(cb + 1) * PROJ_TN] = (g * _sigmoid(g)).astype(BF16)
    for cb in range(LRU_WIDTH // PROJ_TN):
        acc = proj(COL_XR + cb * PROJ_TN, PROJ_TN)
        for q in range(PROJ_TN // LANES):
            n = cb * (PROJ_TN // LANES) + q
            xr_ref[pl.ds(n, tm, stride=LRU_BLOCKS), :] = acc[:, q * LANES:(q + 1) * LANES]


def _in_proj(x2, norm_w, w_in_bf16, q_norm_w, k_norm_w, row_cos, row_sin, col_cos, col_sin):
    S = x2.shape[0]
    tm = PROJ_TM
    assert tm % GRID_W == 0 and S % tm == 0
    const = lambda i: (0, 0)
    est = (2 * tm * D_MODEL * 4 + D_MODEL * IN_WIDTH * 2 + 4 * tm * HEAD_DIM * 4
           + 2 * (3 * tm * ATTN_WIDTH * 2 + 2 * tm * KV_WIDTH * 2 + tm * LRU_WIDTH * 4)
           + tm * D_MODEL * 6 + 4 * tm * PROJ_TN * 4)
    return pl.pallas_call(
        _in_proj_kernel,
        grid=(S // tm,),
        in_specs=[
            pl.BlockSpec((tm, D_MODEL), lambda i: (i, 0)),
            pl.BlockSpec((1, D_MODEL), const),
            pl.BlockSpec((D_MODEL, IN_WIDTH), const, pipeline_mode=pl.Buffered(1)),
            pl.BlockSpec((1, HEAD_DIM), const),
            pl.BlockSpec((1, HEAD_DIM), const),
            pl.BlockSpec((S // GRID_W, HEAD_DIM), const),
            pl.BlockSpec((S // GRID_W, HEAD_DIM), const),
            pl.BlockSpec((tm, HEAD_DIM), const),
            pl.BlockSpec((tm, HEAD_DIM), const),
        ],
        out_specs=[
            pl.BlockSpec((N_KV_HEADS, 1, HEAD_DIM, GROUP * tm), lambda i: (0, i, 0, 0)),
            pl.BlockSpec((N_KV_HEADS, tm, HEAD_DIM), lambda i: (0, i, 0)),
            pl.BlockSpec((N_KV_HEADS, HEAD_DIM, tm), lambda i: (0, 0, i)),
            pl.BlockSpec((N_KV_HEADS * KN_ROWS, LANES), const),
            pl.BlockSpec((tm, ATTN_WIDTH), lambda i: (i, 0)),
            pl.BlockSpec((tm * LRU_BLOCKS, LANES), lambda i: (i, 0)),
            pl.BlockSpec((tm, LRU_WIDTH), lambda i: (i, 0)),
        ],
        out_shape=[
            jax.ShapeDtypeStruct((N_KV_HEADS, S // tm, HEAD_DIM, GROUP * tm), BF16),
            jax.ShapeDtypeStruct((N_KV_HEADS, S, HEAD_DIM), BF16),
            jax.ShapeDtypeStruct((N_KV_HEADS, HEAD_DIM, S), BF16),
            jax.ShapeDtypeStruct((N_KV_HEADS * KN_ROWS, LANES), F32),
            jax.ShapeDtypeStruct((S, ATTN_WIDTH), BF16),
            jax.ShapeDtypeStruct((S * LRU_BLOCKS, LANES), F32),
            jax.ShapeDtypeStruct((S, LRU_WIDTH), BF16),
        ],
        compiler_params=pltpu.CompilerParams(
            dimension_semantics=("arbitrary",), vmem_limit_bytes=_vmem_limit(est)),
        name="in_proj",
    )(x2, norm_w, w_in_bf16, q_norm_w, k_norm_w, row_cos, row_sin, col_cos, col_sin)


ATTN_Q_TILES = 2
ATTN_TK = 512
TILES_PER_TRIP = 6


SOFTMAX_ROWS = 32


def _sublane_allmax(x):
    for shift in (4, 2, 1):
        x = jnp.maximum(x, pltpu.roll(x, shift, 0))
    return x


SAFE_LOGIT_BOUND = 55.0
BOUND_SLACK = 1.0 + 2.0 ** -10
LANE_BLOCK = 256


def _attn_kernel(kn_ref, q_tiles_ref, k_ref, vt_ref, o_ref, qt_ref, m_ref, l_ref, acc_ref, s_buf,
                 tmax_buf, p_buf, alpha_buf, *, tk):
    n_tiles, _, tile_lanes = q_tiles_ref.shape
    tile_rows = tile_lanes // GROUP
    lanes = n_tiles * tile_lanes
    n_kv = k_ref.shape[0] // tk
    n_groups = tk // SUBLANES
    chunk = SOFTMAX_ROWS // SUBLANES
    acc_shape = (HEAD_DIM // SUBLANES, SUBLANES, lanes)

    for t in range(n_tiles):
        qt_ref[:, t * tile_lanes:(t + 1) * tile_lanes] = q_tiles_ref[t]
    qf = qt_ref[...].astype(F32)
    qn2 = jnp.sum(qf * qf, axis=0, keepdims=True)
    kn = kn_ref[...]
    kn2 = jnp.where(pl.program_id(0) == 0, kn[0:1, :], kn[KN_ROWS:KN_ROWS + 1, :])
    bound = jnp.sqrt(qn2 * jnp.tile(kn2, (1, lanes // LANES))) * BOUND_SLACK
    single_pass = jnp.max(bound) <= SAFE_LOGIT_BOUND

    def kv_rows(j):
        return pl.ds(pl.multiple_of(j * tk, tk), tk)

    def finish():
        l_tot = jnp.sum(l_ref[...], axis=0, keepdims=True)
        out_t = acc_ref[...].reshape(HEAD_DIM, lanes) * (1.0 / l_tot)
        for t in range(n_tiles):
            for h in range(GROUP):
                lane0 = t * tile_lanes + h * tile_rows
                o_ref[t * tile_rows:(t + 1) * tile_rows, h * HEAD_DIM:(h + 1) * HEAD_DIM] = (
                    out_t[:, lane0:lane0 + tile_rows].T.astype(o_ref.dtype))

    l_ref[...] = jnp.zeros((SUBLANES, lanes), F32)
    acc_ref[...] = jnp.zeros(acc_shape, F32)

    @pl.when(single_pass)
    def _():
        m_ref[...] = jnp.broadcast_to(bound, (SUBLANES, lanes))

        def probs(j, slot):
            kt = k_ref[kv_rows(j), :]
            for lb in range(lanes // LANE_BLOCK):
                cols = slice(lb * LANE_BLOCK, (lb + 1) * LANE_BLOCK)
                s = jnp.dot(kt, qt_ref[:, cols], preferred_element_type=F32)
                p = jnp.exp2(s.reshape(n_groups, SUBLANES, LANE_BLOCK) - m_ref[:, cols])
                l_ref[:, cols] += jnp.sum(p, axis=0)
                p_buf[slot, :, cols] = p.reshape(tk, LANE_BLOCK).astype(BF16)

        def values(j, slot):
            pv = jnp.dot(vt_ref[:, kv_rows(j)], p_buf[slot], preferred_element_type=F32)
            acc_ref[...] += pv.reshape(acc_shape)

        probs(0, 0)

        def body(t, _):
            base = TILES_PER_TRIP * t
            for u in range(TILES_PER_TRIP):
                probs(base + u + 1, (u + 1) % 2)
                values(base + u, u % 2)
            return 0

        lax.fori_loop(0, (n_kv - 2) // TILES_PER_TRIP, body, 0)
        probs(n_kv - 1, 1)
        values(n_kv - 2, 0)
        values(n_kv - 1, 1)
        finish()

    @pl.when(jnp.logical_not(single_pass))
    def _():
        def scores(j, slot):
            s = jnp.dot(k_ref[kv_rows(j), :], qt_ref[...], preferred_element_type=F32)
            s = s.reshape(n_groups, SUBLANES, lanes)
            s_buf[slot] = s
            tmax_buf[slot] = jnp.max(s, axis=0)

        def softmax(slot):
            m_old = m_ref[...]
            m_new = jnp.maximum(m_old, _sublane_allmax(tmax_buf[slot]))
            alpha = jnp.exp2(m_old - m_new)
            psum = jnp.zeros((SUBLANES, lanes), F32)
            for c in range(n_groups // chunk):
                s = s_buf[slot, c * chunk:(c + 1) * chunk]
                p = jnp.exp2(s - m_new)
                psum = psum + jnp.sum(p, axis=0)
                p_buf[slot, c * SOFTMAX_ROWS:(c + 1) * SOFTMAX_ROWS, :] = (
                    p.reshape(SOFTMAX_ROWS, lanes).astype(BF16))
            l_ref[...] = alpha * l_ref[...] + psum
            m_ref[...] = m_new
            alpha_buf[slot] = alpha

        def values(j, slot):
            pv = jnp.dot(vt_ref[:, kv_rows(j)], p_buf[slot], preferred_element_type=F32)
            acc_ref[...] = alpha_buf[slot] * acc_ref[...] + pv.reshape(acc_shape)

        m_ref[...] = jnp.full((SUBLANES, lanes), NEG_BIG, F32)
        p_buf[1] = jnp.zeros((tk, lanes), BF16)
        alpha_buf[1] = jnp.ones((SUBLANES, lanes), F32)
        scores(0, 0)

        def body(t, _):
            for slot in range(2):
                cur = 2 * t + slot
                nxt = jnp.where(cur + 1 < n_kv, cur + 1, 0)
                prv = jnp.maximum(cur - 1, 0)
                scores(nxt, 1 - slot)
                softmax(slot)
                values(prv, 1 - slot)
            return 0

        lax.fori_loop(0, n_kv // 2, body, 0)
        values(n_kv - 1, 1)
        finish()


def _attention(k_norm2, q_t, k, v_t):
    S = k.shape[1]
    tk = ATTN_TK
    n_tiles = ATTN_Q_TILES
    tile_lanes = q_t.shape[-1]
    tq = n_tiles * (tile_lanes // GROUP)
    assert S % (2 * tk) == 0 and tk % SOFTMAX_ROWS == 0 and S % tq == 0
    assert TILES_PER_TRIP % 2 == 0 and (S // tk - 2) % TILES_PER_TRIP == 0
    lanes = n_tiles * tile_lanes
    est = (3 * HEAD_DIM * lanes * 2 + 2 * 2 * S * HEAD_DIM * 2 + 2 * tq * GROUP * HEAD_DIM * 2
           + (HEAD_DIM + 6 * SUBLANES) * lanes * 4 + 2 * tk * lanes * 6 + 2 * tk * lanes * 4)
    return pl.pallas_call(
        functools.partial(_attn_kernel, tk=tk),
        grid=(N_KV_HEADS, S // tq),
        in_specs=[
            pl.BlockSpec((N_KV_HEADS * KN_ROWS, LANES), lambda g, i: (0, 0)),
            pl.BlockSpec((None, n_tiles, HEAD_DIM, tile_lanes), lambda g, i: (g, i, 0, 0)),
            pl.BlockSpec((None, S, HEAD_DIM), lambda g, i: (g, 0, 0)),
            pl.BlockSpec((None, HEAD_DIM, S), lambda g, i: (g, 0, 0)),
        ],
        out_specs=pl.BlockSpec((tq, GROUP * HEAD_DIM), lambda g, i: (i, g)),
        out_shape=jax.ShapeDtypeStruct((S, ATTN_WIDTH), BF16),
        scratch_shapes=[
            pltpu.VMEM((HEAD_DIM, lanes), BF16),
            pltpu.VMEM((SUBLANES, lanes), F32),
            pltpu.VMEM((SUBLANES, lanes), F32),
            pltpu.VMEM((HEAD_DIM // SUBLANES, SUBLANES, lanes), F32),
            pltpu.VMEM((2, tk // SUBLANES, SUBLANES, lanes), F32),
            pltpu.VMEM((2, SUBLANES, lanes), F32),
            pltpu.VMEM((2, tk, lanes), BF16),
            pltpu.VMEM((2, SUBLANES, lanes), F32),
        ],
        compiler_params=pltpu.CompilerParams(
            dimension_semantics=("arbitrary", "arbitrary"), vmem_limit_bytes=_vmem_limit(est)),
        name="attention",
    )(k_norm2, q_t, k, v_t)


LRU_T = 256
SCAN_UNROLL = 16
assert LRU_WIDTH == SUBLANES * LANES and LRU_BLOCKS == SUBLANES


def _lru_gates(prev_ref, cur_ref, next_ref, ext_tm, xc_tm, a_tm, b_tm, has_prev, has_next,
               cw_ref, cb_ref, w_ref, bias_ref, lam_ref):
    rows = cur_ref.shape[0]
    T = rows // LRU_BLOCKS
    halo = prev_ref.shape[0]
    ext_tm[0:halo, :] = jnp.where(has_prev, prev_ref[...], 0.0)
    ext_tm[halo:halo + rows, :] = cur_ref[...]
    ext_tm[halo + rows:2 * halo + rows, :] = jnp.where(has_next, next_ref[...], 0.0)
    base = halo - CONV_LEFT * LRU_BLOCKS
    xh = cb_ref[...]
    for j in range(CONV_W):
        start = base + j * LRU_BLOCKS
        tap = ext_tm[start:start + rows, :].reshape(T, LRU_BLOCKS, LRU_BW)
        xh = xh + cw_ref[j] * tap
    xc_tm[...] = xh.reshape(rows, LRU_BW)
    lam = lam_ref[...]
    neg = -lam
    softplus = jnp.maximum(neg, 0.0) + jnp.log1p(jnp.exp(-jnp.abs(neg)))
    half_rate = (-0.5 * LRU_C * LOG2_E) * softplus
    for n in range(LRU_BLOCKS):
        cols = slice(n * LRU_BW, (n + 1) * LRU_BW)
        block_n = pl.ds(n, T, stride=LRU_BLOCKS)
        xb = xc_tm[block_n, :]
        z = jnp.dot(xb.astype(BF16), w_ref[n], preferred_element_type=F32)
        tanh_r = jnp.tanh(z[:, :LRU_BW] + bias_ref[0:1, cols])
        tanh_i = jnp.tanh(z[:, LRU_BW:] + bias_ref[1:2, cols])
        rate = half_rate[:, cols]
        a = jnp.exp2(tanh_r * rate + rate)
        one_minus_a2 = 1.0 - a * a
        mult = one_minus_a2 * lax.rsqrt(jnp.maximum(one_minus_a2, TINY))
        half_in = mult * xb
        a_tm[block_n, :] = a
        b_tm[block_n, :] = half_in * tanh_i + half_in


def _lru_scan(af_tm, bf_tm, ab_tm, bb_tm, hf_ref, hb_ref, carry_ref):
    T = af_tm.shape[0] // LRU_BLOCKS
    U = SCAN_UNROLL

    def store(out_ref, steps, t0):
        block = jnp.swapaxes(jnp.stack(steps), 0, 1)
        rows = pl.ds(pl.multiple_of(t0, U), U)
        for n in range(LRU_BLOCKS):
            out_ref[rows, n * LRU_BW:(n + 1) * LRU_BW] = block[n].astype(out_ref.dtype)

    def body(i, carry):
        h_f, h_b = carry
        fwd, bwd = [], []
        for u in range(U):
            t = i * U + u
            rf = pl.ds(pl.multiple_of(t * LRU_BLOCKS, LRU_BLOCKS), LRU_BLOCKS)
            rb = pl.ds(pl.multiple_of((T - 1 - t) * LRU_BLOCKS, LRU_BLOCKS), LRU_BLOCKS)
            h_f = af_tm[rf, :] * h_f + bf_tm[rf, :]
            h_b = ab_tm[rb, :] * h_b + bb_tm[rb, :]
            fwd.append(h_f)
            bwd.append(h_b)
        store(hf_ref, fwd, i * U)
        store(hb_ref, bwd[::-1], T - U - i * U)
        return h_f, h_b

    h_f, h_b = lax.fori_loop(0, T // U, body, (carry_ref[0], carry_ref[1]))
    carry_ref[0] = h_f
    carry_ref[1] = h_b


def _lru_kernel(fp_ref, fc_ref, fn_ref, bp_ref, bc_ref, bn_ref, cw_ref, cb_ref, w_ref, bias_ref,
                lam_ref, hf_ref, hb_ref, ext_tm, xc_tm, af_tm, bf_tm, ab_tm, bb_tm, carry_ref):
    c = pl.program_id(0)
    n_chunks = pl.num_programs(0)

    @pl.when(c == 0)
    def _():
        carry_ref[...] = jnp.zeros(carry_ref.shape, F32)

    _lru_gates(fp_ref, fc_ref, fn_ref, ext_tm, xc_tm, af_tm, bf_tm, c > 0, c < n_chunks - 1,
               cw_ref, cb_ref, w_ref.at[0], bias_ref.at[0], lam_ref.at[0])
    _lru_gates(bp_ref, bc_ref, bn_ref, ext_tm, xc_tm, ab_tm, bb_tm, c < n_chunks - 1, c > 0,
               cw_ref, cb_ref, w_ref.at[1], bias_ref.at[1], lam_ref.at[1])
    _lru_scan(af_tm, bf_tm, ab_tm, bb_tm, hf_ref, hb_ref, carry_ref)


LRU_HALO = SUBLANES


def _lru(xr_tm, conv_w_tm, conv_b_tm, w_gate, bias_gate, lam):
    S = xr_tm.shape[0] // LRU_BLOCKS
    T = LRU_T
    assert S % T == 0 and T % SCAN_UNROLL == 0 and T % LRU_HALO == 0
    assert CONV_LEFT <= LRU_HALO and CONV_W - 1 - CONV_LEFT <= LRU_HALO
    time_major = pltpu.VMEM((T * LRU_BLOCKS, LRU_BW), F32)
    n_chunks = S // T
    per = T // LRU_HALO
    last_blk = S // LRU_HALO - 1
    W = LRU_WIDTH
    chunk = (T * LRU_BLOCKS, LRU_BW)
    halo = (LRU_HALO * LRU_BLOCKS, LRU_BW)
    const2 = lambda c: (0, 0)
    est = (2 * 2 * (T * W * 4 + 2 * SUBLANES * W * 4) + 2 * 2 * T * W * 4
           + 2 * 2 * LRU_BLOCKS * LRU_BW * 2 * LRU_BW * 2 + (6 * T + 64) * W * 4 + 6 * T * W * 4)
    return pl.pallas_call(
        _lru_kernel,
        grid=(n_chunks,),
        in_specs=[
            pl.BlockSpec(halo, lambda c: (jnp.maximum(c * per - 1, 0), 0)),
            pl.BlockSpec(chunk, lambda c: (c, 0)),
            pl.BlockSpec(halo, lambda c: (jnp.minimum((c + 1) * per, last_blk), 0)),
            pl.BlockSpec(halo, lambda c: (jnp.maximum((n_chunks - 1 - c) * per - 1, 0), 0)),
            pl.BlockSpec(chunk, lambda c: (n_chunks - 1 - c, 0)),
            pl.BlockSpec(halo, lambda c: (jnp.minimum((n_chunks - c) * per, last_blk), 0)),
            pl.BlockSpec((CONV_W, LRU_BLOCKS, LRU_BW), lambda c: (0, 0, 0)),
            pl.BlockSpec((LRU_BLOCKS, LRU_BW), const2),
            pl.BlockSpec((2, LRU_BLOCKS, LRU_BW, 2 * LRU_BW), lambda c: (0, 0, 0, 0)),
            pl.BlockSpec((2, 2, W), lambda c: (0, 0, 0)),
            pl.BlockSpec((2, 1, W), lambda c: (0, 0, 0)),
        ],
        out_specs=[
            pl.BlockSpec((T, W), lambda c: (c, 0)),
            pl.BlockSpec((T, W), lambda c: (n_chunks - 1 - c, 0)),
        ],
        out_shape=[jax.ShapeDtypeStruct((S, W), BF16), jax.ShapeDtypeStruct((S, W), BF16)],
        scratch_shapes=[
            pltpu.VMEM(((T + 2 * LRU_HALO) * LRU_BLOCKS, LRU_BW), F32),
            time_major,
            time_major, time_major,
            time_major, time_major,
            pltpu.VMEM((2, SUBLANES, LRU_BW), F32),
        ],
        compiler_params=pltpu.CompilerParams(
            dimension_semantics=("arbitrary",), vmem_limit_bytes=_vmem_limit(est)),
        name="rg_lru",
    )(xr_tm, xr_tm, xr_tm, xr_tm, xr_tm, xr_tm, conv_w_tm, conv_b_tm, w_gate, bias_gate, lam)


OUT_TM = 512


def _out_proj_kernel(ao_ref, ga_ref, hf_ref, hb_ref, gl_ref, x_ref, anw_ref, lnw_ref, w_ref,
                     o_ref):
    def inv_rms(v):
        return lax.rsqrt(jnp.mean(v * v, axis=-1, keepdims=True) + EPS)

    ao = ao_ref[...].astype(F32)
    hl = hf_ref[...].astype(F32) + hb_ref[...].astype(F32)
    ya = (ao * anw_ref[...] * ga_ref[...].astype(F32)).astype(BF16)
    yl = (hl * lnw_ref[...] * gl_ref[...].astype(F32)).astype(BF16)
    mix_a = jnp.dot(ya, w_ref[0:ATTN_WIDTH, :], preferred_element_type=F32) * inv_rms(ao)
    mix_l = jnp.dot(yl, w_ref[ATTN_WIDTH:MIX_WIDTH, :], preferred_element_type=F32) * inv_rms(hl)
    o_ref[...] = x_ref[...] + mix_a + mix_l


def _out_proj(attn_out, g_attn, h_fwd, h_bwd, g_lru, x2, attn_norm_w, lru_norm_w, w_out_bf16):
    S = x2.shape[0]
    tm = OUT_TM
    row = lambda i: (i, 0)
    const = lambda i: (0, 0)
    est = (2 * 5 * tm * ATTN_WIDTH * 2 + 2 * 2 * tm * D_MODEL * 4 + MIX_WIDTH * D_MODEL * 2
           + 4 * tm * D_MODEL * 4)
    return pl.pallas_call(
        _out_proj_kernel,
        grid=(S // tm,),
        in_specs=[
            pl.BlockSpec((tm, ATTN_WIDTH), row),
            pl.BlockSpec((tm, ATTN_WIDTH), row),
            pl.BlockSpec((tm, LRU_WIDTH), row),
            pl.BlockSpec((tm, LRU_WIDTH), row),
            pl.BlockSpec((tm, LRU_WIDTH), row),
            pl.BlockSpec((tm, D_MODEL), row),
            pl.BlockSpec((1, ATTN_WIDTH), const),
            pl.BlockSpec((1, LRU_WIDTH), const),
            pl.BlockSpec((MIX_WIDTH, D_MODEL), const, pipeline_mode=pl.Buffered(1)),
        ],
        out_specs=pl.BlockSpec((tm, D_MODEL), row),
        out_shape=jax.ShapeDtypeStruct((S, D_MODEL), F32),
        compiler_params=pltpu.CompilerParams(
            dimension_semantics=("arbitrary",), vmem_limit_bytes=_vmem_limit(est)),
        name="out_proj",
    )(attn_out, g_attn, h_fwd, h_bwd, g_lru, x2, attn_norm_w, lru_norm_w, w_out_bf16)


def _rope_tables(seq_len, tile_rows):
    rows = seq_len // GRID_W
    inv_freq = ROPE_THETA ** (-jnp.arange(ROPE_PAIRS, dtype=F32) / ROPE_PAIRS)
    ang_r = jnp.arange(rows, dtype=F32)[:, None] * inv_freq[None, :]
    ang_c = jnp.arange(GRID_W, dtype=F32)[:, None] * inv_freq[None, :]
    cr, sr, cc, sc = jnp.cos(ang_r), jnp.sin(ang_r), jnp.cos(ang_c), jnp.sin(ang_c)
    zr, zc = jnp.zeros_like(cr), jnp.zeros_like(cc)
    reps = tile_rows // GRID_W
    return (jnp.concatenate([cr, cr, zr, zr], axis=-1),
            jnp.concatenate([-sr, sr, zr, zr], axis=-1),
            jnp.tile(jnp.concatenate([zc, zc, cc, cc], axis=-1), (reps, 1)),
            jnp.tile(jnp.concatenate([zc, zc, -sc, sc], axis=-1), (reps, 1)))


def _layer(x2, norm_w, w_in, q_norm_w, k_norm_w, conv_w, conv_b, lru_wa, lru_ba, lru_wx, lru_bx,
           lru_lambda, attn_norm_w, lru_norm_w, w_out):
    S = x2.shape[0]
    q_t, k, v_t, k_norm2, g_attn, xr, g_lru = _in_proj(
        x2, norm_w.reshape(1, D_MODEL), w_in.astype(BF16), q_norm_w.reshape(1, HEAD_DIM),
        k_norm_w.reshape(1, HEAD_DIM), *_rope_tables(S, PROJ_TM))
    attn_out = _attention(k_norm2, q_t, k, v_t)
    w_gate = jnp.concatenate([lru_wa, lru_wx], axis=-1).astype(BF16)
    bias_gate = 0.5 * jnp.stack([lru_ba, lru_bx], axis=1)
    h_fwd, h_bwd = _lru(xr, 0.5 * conv_w.reshape(CONV_W, LRU_BLOCKS, LRU_BW),
                        0.5 * conv_b.reshape(LRU_BLOCKS, LRU_BW),
                        w_gate, bias_gate, lru_lambda.reshape(2, 1, LRU_WIDTH))
    return _out_proj(attn_out, g_attn, h_fwd, h_bwd, g_lru, x2,
                     attn_norm_w.reshape(1, ATTN_WIDTH), lru_norm_w.reshape(1, LRU_WIDTH),
                     w_out.astype(BF16))


def kernel(x, norm_w, w_in, q_norm_w, k_norm_w, conv_w, conv_b, lru_wa, lru_ba, lru_wx, lru_bx,
           lru_lambda, attn_norm_w, lru_norm_w, w_out):
    B, S, D = x.shape
    assert D == D_MODEL and S % GRID_W == 0
    outs = [_layer(x[b], norm_w, w_in, q_norm_w, k_norm_w, conv_w, conv_b, lru_wa, lru_ba,
                   lru_wx, lru_bx, lru_lambda, attn_norm_w, lru_norm_w, w_out)
            for b in range(B)]
    return jnp.stack(outs, axis=0) if B > 1 else outs[0][None]
```

```python
import functools
import math

import jax
import jax.numpy as jnp
from jax import lax
from jax.experimental import pallas as pl
from jax.experimental.pallas import tpu as pltpu

F32 = jnp.float32
BF16 = jnp.bfloat16

D_MODEL = 2048
N_HEADS = 8
N_KV_HEADS = 2
GROUP = N_HEADS // N_KV_HEADS
HEAD_DIM = 128
ATTN_WIDTH = N_HEADS * HEAD_DIM
KV_WIDTH = N_KV_HEADS * HEAD_DIM
ROPE_THETA = 10000.0
ROPE_PAIRS = HEAD_DIM // 4
GRID_W = 64
LRU_WIDTH = D_MODEL // 2
LRU_BLOCKS = 8
LRU_BW = LRU_WIDTH // LRU_BLOCKS
LRU_C = 8.0
CONV_W = 4
CONV_LEFT = 2
MIX_WIDTH = ATTN_WIDTH + LRU_WIDTH
IN_WIDTH = 2 * ATTN_WIDTH + 2 * KV_WIDTH + 2 * LRU_WIDTH
EPS = 1e-6

COL_Q = 0
COL_K = ATTN_WIDTH
COL_V = ATTN_WIDTH + KV_WIDTH
COL_GA = ATTN_WIDTH + 2 * KV_WIDTH
COL_XR = COL_GA + ATTN_WIDTH
COL_GL = COL_XR + LRU_WIDTH

SUBLANES = 8
LANES = 128
VMEM_LIMIT_CAP = 60000 * 1024
MIB = 1024 * 1024

NEG_BIG = -1e30
TINY = 1e-30
LOG2_E = math.log2(math.e)


def _vmem_limit(estimate_bytes):
    return int(min(VMEM_LIMIT_CAP, estimate_bytes + 8 * MIB))


def _sigmoid(x):
    return 0.5 * jnp.tanh(0.5 * x) + 0.5


PROJ_TM = 256
PROJ_TN = 512
KN_ROWS = SUBLANES // N_KV_HEADS


def _in_proj_kernel(x_ref, nw_ref, w_ref, qw_ref, kw_ref, rcos_ref, rsin_ref, ccos_ref, csin_ref,
                    q_ref, k_ref, v_ref, kn_ref, ga_ref, xr_ref, gl_ref):
    @pl.when(pl.program_id(0) == 0)
    def _():
        kn_ref[...] = jnp.zeros(kn_ref.shape, F32)

    x = x_ref[...]
    ms = jnp.mean(x * x, axis=-1, keepdims=True)
    inv_rms = lax.rsqrt(ms + EPS)
    h = (x * nw_ref[...]).astype(BF16)

    def rope_table(row_ref, col_ref):
        grid_rows = x_ref.shape[0] // GRID_W
        first = pl.program_id(0) * grid_rows
        lines = [jnp.broadcast_to(row_ref[pl.ds(first + r, 1), :], (GRID_W, HEAD_DIM))
                 for r in range(grid_rows)]
        return jnp.concatenate(lines, axis=0) + col_ref[...]

    cos = rope_table(rcos_ref, ccos_ref)
    sin = rope_table(rsin_ref, csin_ref)
    lane = lax.broadcasted_iota(jnp.int32, (1, HEAD_DIM), 1)
    low_half = (lane % (2 * ROPE_PAIRS)) < ROPE_PAIRS

    def proj(col, width):
        return jnp.dot(h, w_ref[:, col:col + width], preferred_element_type=F32) * inv_rms

    def norm_rope(seg, gain, scale):
        ms_h = jnp.mean(seg * seg, axis=-1, keepdims=True)
        y = seg * lax.rsqrt(ms_h + EPS) * gain
        partner = jnp.where(low_half, pltpu.roll(y, HEAD_DIM - ROPE_PAIRS, 1),
                            pltpu.roll(y, ROPE_PAIRS, 1))
        return (y * cos + partner * sin) * scale

    scale = HEAD_DIM ** -0.5 * LOG2_E
    qw = qw_ref[...]
    kw = kw_ref[...]
    tm = x_ref.shape[0]
    for cb in range(ATTN_WIDTH // PROJ_TN):
        acc = proj(COL_Q + cb * PROJ_TN, PROJ_TN)
        for hh in range(PROJ_TN // HEAD_DIM):
            head = cb * (PROJ_TN // HEAD_DIM) + hh
            seg = acc[:, hh * HEAD_DIM:(hh + 1) * HEAD_DIM]
            qt = norm_rope(seg, qw, scale).T.astype(BF16)
            lane0 = (head % GROUP) * tm
            q_ref[head // GROUP, 0, :, lane0:lane0 + tm] = qt
    acc = proj(COL_K, 2 * KV_WIDTH)
    for hh in range(N_KV_HEADS):
        kb = norm_rope(acc[:, hh * HEAD_DIM:(hh + 1) * HEAD_DIM], kw, 1.0).astype(BF16)
        k_ref[hh] = kb
        kf = kb.astype(F32)
        n2 = jnp.max(jnp.sum(kf * kf, axis=-1, keepdims=True), axis=0, keepdims=True)
        rows = slice(hh * KN_ROWS, (hh + 1) * KN_ROWS)
        kn_ref[rows, :] = jnp.maximum(kn_ref[rows, :], n2)
        vseg = acc[:, KV_WIDTH + hh * HEAD_DIM:KV_WIDTH + (hh + 1) * HEAD_DIM]
        v_ref[hh] = vseg.T.astype(BF16)
    for cb in range(ATTN_WIDTH // PROJ_TN):
        g = proj(COL_GA + cb * PROJ_TN, PROJ_TN)
        ga_ref[:, cb * PROJ_TN:(cb + 1) * PROJ_TN] = (g * _sigmoid(g)).astype(BF16)
    for cb in range(LRU_WIDTH // PROJ_TN):
        g = proj(COL_GL + cb * PROJ_TN, PROJ_TN)
        gl_ref[:, cb * PROJ_TN:(cb + 1) * PROJ_TN] = (g * _sigmoid(g)).astype(BF16)
    for cb in range(LRU_WIDTH // PROJ_TN):
        acc = proj(COL_XR + cb * PROJ_TN, PROJ_TN)
        for q in range(PROJ_TN // LANES):
            n = cb * (PROJ_TN // LANES) + q
            xr_ref[pl.ds(n, tm, stride=LRU_BLOCKS), :] = acc[:, q * LANES:(q + 1) * LANES]


def _in_proj(x2, norm_w, w_in_bf16, q_norm_w, k_norm_w, row_cos, row_sin, col_cos, col_sin):
    S = x2.shape[0]
    tm = PROJ_TM
    assert tm % GRID_W == 0 and S % tm == 0
    const = lambda i: (0, 0)
    est = (2 * tm * D_MODEL * 4 + D_MODEL * IN_WIDTH * 2 + 4 * tm * HEAD_DIM * 4
           + 2 * (3 * tm * ATTN_WIDTH * 2 + 2 * tm * KV_WIDTH * 2 + tm * LRU_WIDTH * 4)
           + tm * D_MODEL * 6 + 4 * tm * PROJ_TN * 4)
    return pl.pallas_call(
        _in_proj_kernel,
        grid=(S // tm,),
        in_specs=[
            pl.BlockSpec((tm, D_MODEL), lambda i: (i, 0)),
            pl.BlockSpec((1, D_MODEL), const),
            pl.BlockSpec((D_MODEL, IN_WIDTH), const, pipeline_mode=pl.Buffered(1)),
            pl.BlockSpec((1, HEAD_DIM), const),
            pl.BlockSpec((1, HEAD_DIM), const),
            pl.BlockSpec((S // GRID_W, HEAD_DIM), const),
            pl.BlockSpec((S // GRID_W, HEAD_DIM), const),
            pl.BlockSpec((tm, HEAD_DIM), const),
            pl.BlockSpec((tm, HEAD_DIM), const),
        ],
        out_specs=[
            pl.BlockSpec((N_KV_HEADS, 1, HEAD_DIM, GROUP * tm), lambda i: (0, i, 0, 0)),
            pl.BlockSpec((N_KV_HEADS, tm, HEAD_DIM), lambda i: (0, i, 0)),
            pl.BlockSpec((N_KV_HEADS, HEAD_DIM, tm), lambda i: (0, 0, i)),
            pl.BlockSpec((N_KV_HEADS * KN_ROWS, LANES), const),
            pl.BlockSpec((tm, ATTN_WIDTH), lambda i: (i, 0)),
            pl.BlockSpec((tm * LRU_BLOCKS, LANES), lambda i: (i, 0)),
            pl.BlockSpec((tm, LRU_WIDTH), lambda i: (i, 0)),
        ],
        out_shape=[
            jax.ShapeDtypeStruct((N_KV_HEADS, S // tm, HEAD_DIM, GROUP * tm), BF16),
            jax.ShapeDtypeStruct((N_KV_HEADS, S, HEAD_DIM), BF16),
            jax.ShapeDtypeStruct((N_KV_HEADS, HEAD_DIM, S), BF16),
            jax.ShapeDtypeStruct((N_KV_HEADS * KN_ROWS, LANES), F32),
            jax.ShapeDtypeStruct((S, ATTN_WIDTH), BF16),
            jax.ShapeDtypeStruct((S * LRU_BLOCKS, LANES), F32),
            jax.ShapeDtypeStruct((S, LRU_WIDTH), BF16),
        ],
        compiler_params=pltpu.CompilerParams(
            dimension_semantics=("arbitrary",), vmem_limit_bytes=_vmem_limit(est)),
        name="in_proj",
    )(x2, norm_w, w_in_bf16, q_norm_w, k_norm_w, row_cos, row_sin, col_cos, col_sin)


ATTN_Q_TILES = 2
ATTN_TK = 1024
TILES_PER_TRIP = 2


SOFTMAX_ROWS = 32


def _sublane_allmax(x):
    for shift in (4, 2, 1):
        x = jnp.maximum(x, pltpu.roll(x, shift, 0))
    return x


SAFE_LOGIT_BOUND = 0.0
BOUND_SLACK = 1.0 + 2.0 ** -10
LANE_BLOCK = 256


def _attn_kernel(kn_ref, q_tiles_ref, k_ref, vt_ref, o_ref, qt_ref, m_ref, l_ref, acc_ref, s_buf,
                 tmax_buf, p_buf, alpha_buf, *, tk):
    n_tiles, _, tile_lanes = q_tiles_ref.shape
    tile_rows = tile_lanes // GROUP
    lanes = n_tiles * tile_lanes
    n_kv = k_ref.shape[0] // tk
    n_groups = tk // SUBLANES
    chunk = SOFTMAX_ROWS // SUBLANES
    acc_shape = (HEAD_DIM // SUBLANES, SUBLANES, lanes)

    for t in range(n_tiles):
        qt_ref[:, t * tile_lanes:(t + 1) * tile_lanes] = q_tiles_ref[t]
    qf = qt_ref[...].astype(F32)
    qn2 = jnp.sum(qf * qf, axis=0, keepdims=True)
    kn = kn_ref[...]
    kn2 = jnp.where(pl.program_id(0) == 0, kn[0:1, :], kn[KN_ROWS:KN_ROWS + 1, :])
    bound = jnp.sqrt(qn2 * jnp.tile(kn2, (1, lanes // LANES))) * BOUND_SLACK
    single_pass = jnp.max(bound) <= SAFE_LOGIT_BOUND

    def kv_rows(j):
        return pl.ds(pl.multiple_of(j * tk, tk), tk)

    def finish():
        l_tot = jnp.sum(l_ref[...], axis=0, keepdims=True)
        out_t = acc_ref[...].reshape(HEAD_DIM, lanes) * (1.0 / l_tot)
        for t in range(n_tiles):
            for h in range(GROUP):
                lane0 = t * tile_lanes + h * tile_rows
                o_ref[t * tile_rows:(t + 1) * tile_rows, h * HEAD_DIM:(h + 1) * HEAD_DIM] = (
                    out_t[:, lane0:lane0 + tile_rows].T.astype(o_ref.dtype))

    l_ref[...] = jnp.zeros((SUBLANES, lanes), F32)
    acc_ref[...] = jnp.zeros(acc_shape, F32)

    @pl.when(single_pass)
    def _():
        m_ref[...] = jnp.broadcast_to(bound, (SUBLANES, lanes))

        def probs(j, slot):
            kt = k_ref[kv_rows(j), :]
            for lb in range(lanes // LANE_BLOCK):
                cols = slice(lb * LANE_BLOCK, (lb + 1) * LANE_BLOCK)
                s = jnp.dot(kt, qt_ref[:, cols], preferred_element_type=F32)
                p = jnp.exp2(s.reshape(n_groups, SUBLANES, LANE_BLOCK) - m_ref[:, cols])
                l_ref[:, cols] += jnp.sum(p, axis=0)
                p_buf[slot, :, cols] = p.reshape(tk, LANE_BLOCK).astype(BF16)

        def values(j, slot):
            pv = jnp.dot(vt_ref[:, kv_rows(j)], p_buf[slot], preferred_element_type=F32)
            acc_ref[...] += pv.reshape(acc_shape)

        probs(0, 0)

        def body(t, _):
            base = TILES_PER_TRIP * t
            for u in range(TILES_PER_TRIP):
                probs(base + u + 1, (u + 1) % 2)
                values(base + u, u % 2)
            return 0

        lax.fori_loop(0, (n_kv - 2) // TILES_PER_TRIP, body, 0)
        probs(n_kv - 1, 1)
        values(n_kv - 2, 0)
        values(n_kv - 1, 1)
        finish()

    @pl.when(jnp.logical_not(single_pass))
    def _():
        def scores(j, slot):
            s = jnp.dot(k_ref[kv_rows(j), :], qt_ref[...], preferred_element_type=F32)
            s = s.reshape(n_groups, SUBLANES, lanes)
            s_buf[slot] = s
            tmax_buf[slot] = jnp.max(s, axis=0)

        def softmax(slot):
            m_old = m_ref[...]
            m_new = jnp.maximum(m_old, _sublane_allmax(tmax_buf[slot]))
            alpha = jnp.exp2(m_old - m_new)
            psum = jnp.zeros((SUBLANES, lanes), F32)
            for c in range(n_groups // chunk):
                s = s_buf[slot, c * chunk:(c + 1) * chunk]
                p = jnp.exp2(s - m_new)
                psum = psum + jnp.sum(p, axis=0)
                p_buf[slot, c * SOFTMAX_ROWS:(c + 1) * SOFTMAX_ROWS, :] = (
                    p.reshape(SOFTMAX_ROWS, lanes).astype(BF16))
            l_ref[...] = alpha * l_ref[...] + psum
            m_ref[...] = m_new
            alpha_buf[slot] = alpha

        def values(j, slot):
            pv = jnp.dot(vt_ref[:, kv_rows(j)], p_buf[slot], preferred_element_type=F32)
            acc_ref[...] = alpha_buf[slot] * acc_ref[...] + pv.reshape(acc_shape)

        m_ref[...] = jnp.full((SUBLANES, lanes), NEG_BIG, F32)
        p_buf[1] = jnp.zeros((tk, lanes), BF16)
        alpha_buf[1] = jnp.ones((SUBLANES, lanes), F32)
        scores(0, 0)

        def body(t, _):
            for slot in range(2):
                cur = 2 * t + slot
                nxt = jnp.where(cur + 1 < n_kv, cur + 1, 0)
                prv = jnp.maximum(cur - 1, 0)
                scores(nxt, 1 - slot)
                softmax(slot)
                values(prv, 1 - slot)
            return 0

        lax.fori_loop(0, n_kv // 2, body, 0)
        values(n_kv - 1, 1)
        finish()


def _attention(k_norm2, q_t, k, v_t):
    S = k.shape[1]
    tk = ATTN_TK
    n_tiles = ATTN_Q_TILES
    tile_lanes = q_t.shape[-1]
    tq = n_tiles * (tile_lanes // GROUP)
    assert S % (2 * tk) == 0 and tk % SOFTMAX_ROWS == 0 and S % tq == 0
    assert TILES_PER_TRIP % 2 == 0 and (S // tk - 2) % TILES_PER_TRIP == 0
    lanes = n_tiles * tile_lanes
    est = (3 * HEAD_DIM * lanes * 2 + 2 * 2 * S * HEAD_DIM * 2 + 2 * tq * GROUP * HEAD_DIM * 2
           + (HEAD_DIM + 6 * SUBLANES) * lanes * 4 + 2 * tk * lanes * 6 + 2 * tk * lanes * 4)
    return pl.pallas_call(
        functools.partial(_attn_kernel, tk=tk),
        grid=(N_KV_HEADS, S // tq),
        in_specs=[
            pl.BlockSpec((N_KV_HEADS * KN_ROWS, LANES), lambda g, i: (0, 0)),
            pl.BlockSpec((None, n_tiles, HEAD_DIM, tile_lanes), lambda g, i: (g, i, 0, 0)),
            pl.BlockSpec((None, S, HEAD_DIM), lambda g, i: (g, 0, 0)),
            pl.BlockSpec((None, HEAD_DIM, S), lambda g, i: (g, 0, 0)),
        ],
        out_specs=pl.BlockSpec((tq, GROUP * HEAD_DIM), lambda g, i: (i, g)),
        out_shape=jax.ShapeDtypeStruct((S, ATTN_WIDTH), BF16),
        scratch_shapes=[
            pltpu.VMEM((HEAD_DIM, lanes), BF16),
            pltpu.VMEM((SUBLANES, lanes), F32),
            pltpu.VMEM((SUBLANES, lanes), F32),
            pltpu.VMEM((HEAD_DIM // SUBLANES, SUBLANES, lanes), F32),
            pltpu.VMEM((2, tk // SUBLANES, SUBLANES, lanes), F32),
            pltpu.VMEM((2, SUBLANES, lanes), F32),
            pltpu.VMEM((2, tk, lanes), BF16),
            pltpu.VMEM((2, SUBLANES, lanes), F32),
        ],
        compiler_params=pltpu.CompilerParams(
            dimension_semantics=("arbitrary", "arbitrary"), vmem_limit_bytes=_vmem_limit(est)),
        name="attention",
    )(k_norm2, q_t, k, v_t)


LRU_T = 256
SCAN_UNROLL = 16
assert LRU_WIDTH == SUBLANES * LANES and LRU_BLOCKS == SUBLANES


def _lru_gates(prev_ref, cur_ref, next_ref, ext_tm, xc_tm, a_tm, b_tm, has_prev, has_next,
               cw_ref, cb_ref, w_ref, bias_ref, lam_ref):
    rows = cur_ref.shape[0]
    T = rows // LRU_BLOCKS
    halo = prev_ref.shape[0]
    ext_tm[0:halo, :] = jnp.where(has_prev, prev_ref[...], 0.0)
    ext_tm[halo:halo + rows, :] = cur_ref[...]
    ext_tm[halo + rows:2 * halo + rows, :] = jnp.where(has_next, next_ref[...], 0.0)
    base = halo - CONV_LEFT * LRU_BLOCKS
    xh = cb_ref[...]
    for j in range(CONV_W):
        start = base + j * LRU_BLOCKS
        tap = ext_tm[start:start + rows, :].reshape(T, LRU_BLOCKS, LRU_BW)
        xh = xh + cw_ref[j] * tap
    xc_tm[...] = xh.reshape(rows, LRU_BW)
    lam = lam_ref[...]
    neg = -lam
    softplus = jnp.maximum(neg, 0.0) + jnp.log1p(jnp.exp(-jnp.abs(neg)))
    half_rate = (-0.5 * LRU_C * LOG2_E) * softplus
    for n in range(LRU_BLOCKS):
        cols = slice(n * LRU_BW, (n + 1) * LRU_BW)
        block_n = pl.ds(n, T, stride=LRU_BLOCKS)
        xb = xc_tm[block_n, :]
        z = jnp.dot(xb.astype(BF16), w_ref[n], preferred_element_type=F32)
        tanh_r = jnp.tanh(z[:, :LRU_BW] + bias_ref[0:1, cols])
        tanh_i = jnp.tanh(z[:, LRU_BW:] + bias_ref[1:2, cols])
        rate = half_rate[:, cols]
        a = jnp.exp2(tanh_r * rate + rate)
        one_minus_a2 = 1.0 - a * a
        mult = one_minus_a2 * lax.rsqrt(jnp.maximum(one_minus_a2, TINY))
        half_in = mult * xb
        a_tm[block_n, :] = a
        b_tm[block_n, :] = half_in * tanh_i + half_in


def _lru_scan(af_tm, bf_tm, ab_tm, bb_tm, hf_ref, hb_ref, carry_ref):
    T = af_tm.shape[0] // LRU_BLOCKS
    U = SCAN_UNROLL

    def store(out_ref, steps, t0):
        block = jnp.swapaxes(jnp.stack(steps), 0, 1)
        rows = pl.ds(pl.multiple_of(t0, U), U)
        for n in range(LRU_BLOCKS):
            out_ref[rows, n * LRU_BW:(n + 1) * LRU_BW] = block[n].astype(out_ref.dtype)

    def body(i, carry):
        h_f, h_b = carry
        fwd, bwd = [], []
        for u in range(U):
            t = i * U + u
            rf = pl.ds(pl.multiple_of(t * LRU_BLOCKS, LRU_BLOCKS), LRU_BLOCKS)
            rb = pl.ds(pl.multiple_of((T - 1 - t) * LRU_BLOCKS, LRU_BLOCKS), LRU_BLOCKS)
            h_f = af_tm[rf, :] * h_f + bf_tm[rf, :]
            h_b = ab_tm[rb, :] * h_b + bb_tm[rb, :]
            fwd.append(h_f)
            bwd.append(h_b)
        store(hf_ref, fwd, i * U)
        store(hb_ref, bwd[::-1], T - U - i * U)
        return h_f, h_b

    h_f, h_b = lax.fori_loop(0, T // U, body, (carry_ref[0], carry_ref[1]))
    carry_ref[0] = h_f
    carry_ref[1] = h_b


def _lru_kernel(fp_ref, fc_ref, fn_ref, bp_ref, bc_ref, bn_ref, cw_ref, cb_ref, w_ref, bias_ref,
                lam_ref, hf_ref, hb_ref, ext_tm, xc_tm, af_tm, bf_tm, ab_tm, bb_tm, carry_ref):
    c = pl.program_id(0)
    n_chunks = pl.num_programs(0)

    @pl.when(c == 0)
    def _():
        carry_ref[...] = jnp.zeros(carry_ref.shape, F32)

    _lru_gates(fp_ref, fc_ref, fn_ref, ext_tm, xc_tm, af_tm, bf_tm, c > 0, c < n_chunks - 1,
               cw_ref, cb_ref, w_ref.at[0], bias_ref.at[0], lam_ref.at[0])
    _lru_gates(bp_ref, bc_ref, bn_ref, ext_tm, xc_tm, ab_tm, bb_tm, c < n_chunks - 1, c > 0,
               cw_ref, cb_ref, w_ref.at[1], bias_ref.at[1], lam_ref.at[1])
    _lru_scan(af_tm, bf_tm, ab_tm, bb_tm, hf_ref, hb_ref, carry_ref)


LRU_HALO = SUBLANES


def _lru(xr_tm, conv_w_tm, conv_b_tm, w_gate, bias_gate, lam):
    S = xr_tm.shape[0] // LRU_BLOCKS
    T = LRU_T
    assert S % T == 0 and T % SCAN_UNROLL == 0 and T % LRU_HALO == 0
    assert CONV_LEFT <= LRU_HALO and CONV_W - 1 - CONV_LEFT <= LRU_HALO
    time_major = pltpu.VMEM((T * LRU_BLOCKS, LRU_BW), F32)
    n_chunks = S // T
    per = T // LRU_HALO
    last_blk = S // LRU_HALO - 1
    W = LRU_WIDTH
    chunk = (T * LRU_BLOCKS, LRU_BW)
    halo = (LRU_HALO * LRU_BLOCKS, LRU_BW)
    const2 = lambda c: (0, 0)
    est = (2 * 2 * (T * W * 4 + 2 * SUBLANES * W * 4) + 2 * 2 * T * W * 4
           + 2 * 2 * LRU_BLOCKS * LRU_BW * 2 * LRU_BW * 2 + (6 * T + 64) * W * 4 + 6 * T * W * 4)
    return pl.pallas_call(
        _lru_kernel,
        grid=(n_chunks,),
        in_specs=[
            pl.BlockSpec(halo, lambda c: (jnp.maximum(c * per - 1, 0), 0)),
            pl.BlockSpec(chunk, lambda c: (c, 0)),
            pl.BlockSpec(halo, lambda c: (jnp.minimum((c + 1) * per, last_blk), 0)),
            pl.BlockSpec(halo, lambda c: (jnp.maximum((n_chunks - 1 - c) * per - 1, 0), 0)),
            pl.BlockSpec(chunk, lambda c: (n_chunks - 1 - c, 0)),
            pl.BlockSpec(halo, lambda c: (jnp.minimum((n_chunks - c) * per, last_blk), 0)),
            pl.BlockSpec((CONV_W, LRU_BLOCKS, LRU_BW), lambda c: (0, 0, 0)),
            pl.BlockSpec((LRU_BLOCKS, LRU_BW), const2),
            pl.BlockSpec((2, LRU_BLOCKS, LRU_BW, 2 * LRU_BW), lambda c: (0, 0, 0, 0)),
            pl.BlockSpec((2, 2, W), lambda c: (0, 0, 0)),
            pl.BlockSpec((2, 1, W), lambda c: (0, 0, 0)),
        ],
        out_specs=[
            pl.BlockSpec((T, W), lambda c: (c, 0)),
            pl.BlockSpec((T, W), lambda c: (n_chunks - 1 - c, 0)),
        ],
        out_shape=[jax.ShapeDtypeStruct((S, W), BF16), jax.ShapeDtypeStruct((S, W), BF16)],
        scratch_shapes=[
            pltpu.VMEM(((T + 2 * LRU_HALO) * LRU_BLOCKS, LRU_BW), F32),
            time_major,
            time_major, time_major,
            time_major, time_major,
            pltpu.VMEM((2, SUBLANES, LRU_BW), F32),
        ],
        compiler_params=pltpu.CompilerParams(
            dimension_semantics=("arbitrary",), vmem_limit_bytes=_vmem_limit(est)),
        name="rg_lru",
    )(xr_tm, xr_tm, xr_tm, xr_tm, xr_tm, xr_tm, conv_w_tm, conv_b_tm, w_gate, bias_gate, lam)


OUT_TM = 512


def _out_proj_kernel(ao_ref, ga_ref, hf_ref, hb_ref, gl_ref, x_ref, anw_ref, lnw_ref, w_ref,
                     o_ref):
    def inv_rms(v):
        return lax.rsqrt(jnp.mean(v * v, axis=-1, keepdims=True) + EPS)

    ao = ao_ref[...].astype(F32)
    hl = hf_ref[...].astype(F32) + hb_ref[...].astype(F32)
    ya = (ao * anw_ref[...] * ga_ref[...].astype(F32)).astype(BF16)
    yl = (hl * lnw_ref[...] * gl_ref[...].astype(F32)).astype(BF16)
    mix_a = jnp.dot(ya, w_ref[0:ATTN_WIDTH, :], preferred_element_type=F32) * inv_rms(ao)
    mix_l = jnp.dot(yl, w_ref[ATTN_WIDTH:MIX_WIDTH, :], preferred_element_type=F32) * inv_rms(hl)
    o_ref[...] = x_ref[...] + mix_a + mix_l


def _out_proj(attn_out, g_attn, h_fwd, h_bwd, g_lru, x2, attn_norm_w, lru_norm_w, w_out_bf16):
    S = x2.shape[0]
    tm = OUT_TM
    row = lambda i: (i, 0)
    const = lambda i: (0, 0)
    est = (2 * 5 * tm * ATTN_WIDTH * 2 + 2 * 2 * tm * D_MODEL * 4 + MIX_WIDTH * D_MODEL * 2
           + 4 * tm * D_MODEL * 4)
    return pl.pallas_call(
        _out_proj_kernel,
        grid=(S // tm,),
        in_specs=[
            pl.BlockSpec((tm, ATTN_WIDTH), row),
            pl.BlockSpec((tm, ATTN_WIDTH), row),
            pl.BlockSpec((tm, LRU_WIDTH), row),
            pl.BlockSpec((tm, LRU_WIDTH), row),
            pl.BlockSpec((tm, LRU_WIDTH), row),
            pl.BlockSpec((tm, D_MODEL), row),
            pl.BlockSpec((1, ATTN_WIDTH), const),
            pl.BlockSpec((1, LRU_WIDTH), const),
            pl.BlockSpec((MIX_WIDTH, D_MODEL), const, pipeline_mode=pl.Buffered(1)),
        ],
        out_specs=pl.BlockSpec((tm, D_MODEL), row),
        out_shape=jax.ShapeDtypeStruct((S, D_MODEL), F32),
        compiler_params=pltpu.CompilerParams(
            dimension_semantics=("arbitrary",), vmem_limit_bytes=_vmem_limit(est)),
        name="out_proj",
    )(attn_out, g_attn, h_fwd, h_bwd, g_lru, x2, attn_norm_w, lru_norm_w, w_out_bf16)


def _rope_tables(seq_len, tile_rows):
    rows = seq_len // GRID_W
    inv_freq = ROPE_THETA ** (-jnp.arange(ROPE_PAIRS, dtype=F32) / ROPE_PAIRS)
    ang_r = jnp.arange(rows, dtype=F32)[:, None] * inv_freq[None, :]
    ang_c = jnp.arange(GRID_W, dtype=F32)[:, None] * inv_freq[None, :]
    cr, sr, cc, sc = jnp.cos(ang_r), jnp.sin(ang_r), jnp.cos(ang_c), jnp.sin(ang_c)
    zr, zc = jnp.zeros_like(cr), jnp.zeros_like(cc)
    reps = tile_rows // GRID_W
    return (jnp.concatenate([cr, cr, zr, zr], axis=-1),
            jnp.concatenate([-sr, sr, zr, zr], axis=-1),
            jnp.tile(jnp.concatenate([zc, zc, cc, cc], axis=-1), (reps, 1)),
            jnp.tile(jnp.concatenate([zc, zc, -sc, sc], axis=-1), (reps, 1)))


def _layer(x2, norm_w, w_in, q_norm_w, k_norm_w, conv_w, conv_b, lru_wa, lru_ba, lru_wx, lru_bx,
           lru_lambda, attn_norm_w, lru_norm_w, w_out):
    S = x2.shape[0]
    q_t, k, v_t, k_norm2, g_attn, xr, g_lru = _in_proj(
        x2, norm_w.reshape(1, D_MODEL), w_in.astype(BF16), q_norm_w.reshape(1, HEAD_DIM),
        k_norm_w.reshape(1, HEAD_DIM), *_rope_tables(S, PROJ_TM))
    attn_out = _attention(k_norm2, q_t, k, v_t)
    w_gate = jnp.concatenate([lru_wa, lru_wx], axis=-1).astype(BF16)
    bias_gate = 0.5 * jnp.stack([lru_ba, lru_bx], axis=1)
    h_fwd, h_bwd = _lru(xr, 0.5 * conv_w.reshape(CONV_W, LRU_BLOCKS, LRU_BW),
                        0.5 * conv_b.reshape(LRU_BLOCKS, LRU_BW),
                        w_gate, bias_gate, lru_lambda.reshape(2, 1, LRU_WIDTH))
    return _out_proj(attn_out, g_attn, h_fwd, h_bwd, g_lru, x2,
                     attn_norm_w.reshape(1, ATTN_WIDTH), lru_norm_w.reshape(1, LRU_WIDTH),
                     w_out.astype(BF16))


def kernel(x, norm_w, w_in, q_norm_w, k_norm_w, conv_w, conv_b, lru_wa, lru_ba, lru_wx, lru_bx,
           lru_lambda, attn_norm_w, lru_norm_w, w_out):
    B, S, D = x.shape
    assert D == D_MODEL and S % GRID_W == 0
    outs = [_layer(x[b], norm_w, w_in, q_norm_w, k_norm_w, conv_w, conv_b, lru_wa, lru_ba,
                   lru_wx, lru_bx, lru_lambda, attn_norm_w, lru_norm_w, w_out)
            for b in range(B)]
    return jnp.stack(outs, axis=0) if B > 1 else outs[0][None]
```

```python
import functools
import math

import jax
import jax.numpy as jnp
from jax import lax
from jax.experimental import pallas as pl
from jax.experimental.pallas import tpu as pltpu

F32 = jnp.float32
BF16 = jnp.bfloat16

D_MODEL = 2048
N_HEADS = 8
N_KV_HEADS = 2
GROUP = N_HEADS // N_KV_HEADS
HEAD_DIM = 128
ATTN_WIDTH = N_HEADS * HEAD_DIM
KV_WIDTH = N_KV_HEADS * HEAD_DIM
ROPE_THETA = 10000.0
ROPE_PAIRS = HEAD_DIM // 4
GRID_W = 64
LRU_WIDTH = D_MODEL // 2
LRU_BLOCKS = 8
LRU_BW = LRU_WIDTH // LRU_BLOCKS
LRU_C = 8.0
CONV_W = 4
CONV_LEFT = 2
MIX_WIDTH = ATTN_WIDTH + LRU_WIDTH
IN_WIDTH = 2 * ATTN_WIDTH + 2 * KV_WIDTH + 2 * LRU_WIDTH
EPS = 1e-6

COL_Q = 0
COL_K = ATTN_WIDTH
COL_V = ATTN_WIDTH + KV_WIDTH
COL_GA = ATTN_WIDTH + 2 * KV_WIDTH
COL_XR = COL_GA + ATTN_WIDTH
COL_GL = COL_XR + LRU_WIDTH

SUBLANES = 8
LANES = 128
VMEM_LIMIT_CAP = 60000 * 1024
MIB = 1024 * 1024

NEG_BIG = -1e30
TINY = 1e-30
LOG2_E = math.log2(math.e)


def _vmem_limit(estimate_bytes):
    return int(min(VMEM_LIMIT_CAP, estimate_bytes + 8 * MIB))


def _sigmoid(x):
    return 0.5 * jnp.tanh(0.5 * x) + 0.5


PROJ_TM = 256
PROJ_TN = 512
KN_ROWS = SUBLANES // N_KV_HEADS


def _in_proj_kernel(x_ref, nw_ref, w_ref, qw_ref, kw_ref, rcos_ref, rsin_ref, ccos_ref, csin_ref,
                    q_ref, k_ref, v_ref, kn_ref, ga_ref, xr_ref, gl_ref):
    @pl.when(pl.program_id(0) == 0)
    def _():
        kn_ref[...] = jnp.zeros(kn_ref.shape, F32)

    x = x_ref[...]
    ms = jnp.mean(x * x, axis=-1, keepdims=True)
    inv_rms = lax.rsqrt(ms + EPS)
    h = (x * nw_ref[...]).astype(BF16)

    def rope_table(row_ref, col_ref):
        grid_rows = x_ref.shape[0] // GRID_W
        first = pl.program_id(0) * grid_rows
        lines = [jnp.broadcast_to(row_ref[pl.ds(first + r, 1), :], (GRID_W, HEAD_DIM))
                 for r in range(grid_rows)]
        return jnp.concatenate(lines, axis=0) + col_ref[...]

    cos = rope_table(rcos_ref, ccos_ref)
    sin = rope_table(rsin_ref, csin_ref)
    lane = lax.broadcasted_iota(jnp.int32, (1, HEAD_DIM), 1)
    low_half = (lane % (2 * ROPE_PAIRS)) < ROPE_PAIRS

    def proj(col, width):
        return jnp.dot(h, w_ref[:, col:col + width], preferred_element_type=F32) * inv_rms

    def norm_rope(seg, gain, scale):
        ms_h = jnp.mean(seg * seg, axis=-1, keepdims=True)
        y = seg * lax.rsqrt(ms_h + EPS) * gain
        partner = jnp.where(low_half, pltpu.roll(y, HEAD_DIM - ROPE_PAIRS, 1),
                            pltpu.roll(y, ROPE_PAIRS, 1))
        return (y * cos + partner * sin) * scale

    scale = HEAD_DIM ** -0.5 * LOG2_E
    qw = qw_ref[...]
    kw = kw_ref[...]
    tm = x_ref.shape[0]
    for cb in range(ATTN_WIDTH // PROJ_TN):
        acc = proj(COL_Q + cb * PROJ_TN, PROJ_TN)
        for hh in range(PROJ_TN // HEAD_DIM):
            head = cb * (PROJ_TN // HEAD_DIM) + hh
            seg = acc[:, hh * HEAD_DIM:(hh + 1) * HEAD_DIM]
            qt = norm_rope(seg, qw, scale).T.astype(BF16)
            lane0 = (head % GROUP) * tm
            q_ref[head // GROUP, 0, :, lane0:lane0 + tm] = qt
    acc = proj(COL_K, 2 * KV_WIDTH)
    for hh in range(N_KV_HEADS):
        kb = norm_rope(acc[:, hh * HEAD_DIM:(hh + 1) * HEAD_DIM], kw, 1.0).astype(BF16)
        k_ref[hh] = kb
        kf = kb.astype(F32)
        n2 = jnp.max(jnp.sum(kf * kf, axis=-1, keepdims=True), axis=0, keepdims=True)
        rows = slice(hh * KN_ROWS, (hh + 1) * KN_ROWS)
        kn_ref[rows, :] = jnp.maximum(kn_ref[rows, :], n2)
        vseg = acc[:, KV_WIDTH + hh * HEAD_DIM:KV_WIDTH + (hh + 1) * HEAD_DIM]
        v_ref[hh] = vseg.T.astype(BF16)
    for cb in range(ATTN_WIDTH // PROJ_TN):
        g = proj(COL_GA + cb * PROJ_TN, PROJ_TN)
        ga_ref[:, cb * PROJ_TN:(cb + 1) * PROJ_TN] = (g * _sigmoid(g)).astype(BF16)
    for cb in range(LRU_WIDTH // PROJ_TN):
        g = proj(COL_GL + cb * PROJ_TN, PROJ_TN)
        gl_ref[:, cb * PROJ_TN:(cb + 1) * PROJ_TN] = (g * _sigmoid(g)).astype(BF16)
    for cb in range(LRU_WIDTH // PROJ_TN):
        acc = proj(COL_XR + cb * PROJ_TN, PROJ_TN)
        for q in range(PROJ_TN // LANES):
            n = cb * (PROJ_TN // LANES) + q
            xr_ref[pl.ds(n, tm, stride=LRU_BLOCKS), :] = acc[:, q * LANES:(q + 1) * LANES]


def _in_proj(x2, norm_w, w_in_bf16, q_norm_w, k_norm_w, row_cos, row_sin, col_cos, col_sin):
    S = x2.shape[0]
    tm = PROJ_TM
    assert tm % GRID_W == 0 and S % tm == 0
    const = lambda i: (0, 0)
    est = (2 * tm * D_MODEL * 4 + D_MODEL * IN_WIDTH * 2 + 4 * tm * HEAD_DIM * 4
           + 2 * (3 * tm * ATTN_WIDTH * 2 + 2 * tm * KV_WIDTH * 2 + tm * LRU_WIDTH * 4)
           + tm * D_MODEL * 6 + 4 * tm * PROJ_TN * 4)
    return pl.pallas_call(
        _in_proj_kernel,
        grid=(S // tm,),
        in_specs=[
            pl.BlockSpec((tm, D_MODEL), lambda i: (i, 0)),
            pl.BlockSpec((1, D_MODEL), const),
            pl.BlockSpec((D_MODEL, IN_WIDTH), const, pipeline_mode=pl.Buffered(1)),
            pl.BlockSpec((1, HEAD_DIM), const),
            pl.BlockSpec((1, HEAD_DIM), const),
            pl.BlockSpec((S // GRID_W, HEAD_DIM), const),
            pl.BlockSpec((S // GRID_W, HEAD_DIM), const),
            pl.BlockSpec((tm, HEAD_DIM), const),
            pl.BlockSpec((tm, HEAD_DIM), const),
        ],
        out_specs=[
            pl.BlockSpec((N_KV_HEADS, 1, HEAD_DIM, GROUP * tm), lambda i: (0, i, 0, 0)),
            pl.BlockSpec((N_KV_HEADS, tm, HEAD_DIM), lambda i: (0, i, 0)),
            pl.BlockSpec((N_KV_HEADS, HEAD_DIM, tm), lambda i: (0, 0, i)),
            pl.BlockSpec((N_KV_HEADS * KN_ROWS, LANES), const),
            pl.BlockSpec((tm, ATTN_WIDTH), lambda i: (i, 0)),
            pl.BlockSpec((tm * LRU_BLOCKS, LANES), lambda i: (i, 0)),
            pl.BlockSpec((tm, LRU_WIDTH), lambda i: (i, 0)),
        ],
        out_shape=[
            jax.ShapeDtypeStruct((N_KV_HEADS, S // tm, HEAD_DIM, GROUP * tm), BF16),
            jax.ShapeDtypeStruct((N_KV_HEADS, S, HEAD_DIM), BF16),
            jax.ShapeDtypeStruct((N_KV_HEADS, HEAD_DIM, S), BF16),
            jax.ShapeDtypeStruct((N_KV_HEADS * KN_ROWS, LANES), F32),
            jax.ShapeDtypeStruct((S, ATTN_WIDTH), BF16),
            jax.ShapeDtypeStruct((S * LRU_BLOCKS, LANES), F32),
            jax.ShapeDtypeStruct((S, LRU_WIDTH), BF16),
        ],
        compiler_params=pltpu.CompilerParams(
            dimension_semantics=("arbitrary",), vmem_limit_bytes=_vmem_limit(est)),
        name="in_proj",
    )(x2, norm_w, w_in_bf16, q_norm_w, k_norm_w, row_cos, row_sin, col_cos, col_sin)


ATTN_Q_TILES = 2
ATTN_TK = 1024
TILES_PER_TRIP = 2


SOFTMAX_ROWS = 32


def _sublane_allmax(x):
    for shift in (4, 2, 1):
        x = jnp.maximum(x, pltpu.roll(x, shift, 0))
    return x


SAFE_LOGIT_BOUND = 55.0
BOUND_SLACK = 1.0 + 2.0 ** -10
LANE_BLOCK = 256


def _attn_kernel(kn_ref, q_tiles_ref, k_ref, vt_ref, o_ref, qt_ref, m_ref, l_ref, acc_ref, s_buf,
                 tmax_buf, p_buf, alpha_buf, *, tk):
    n_tiles, _, tile_lanes = q_tiles_ref.shape
    tile_rows = tile_lanes // GROUP
    lanes = n_tiles * tile_lanes
    n_kv = k_ref.shape[0] // tk
    n_groups = tk // SUBLANES
    chunk = SOFTMAX_ROWS // SUBLANES
    acc_shape = (HEAD_DIM // SUBLANES, SUBLANES, lanes)

    for t in range(n_tiles):
        qt_ref[:, t * tile_lanes:(t + 1) * tile_lanes] = q_tiles_ref[t]
    qf = qt_ref[...].astype(F32)
    qn2 = jnp.sum(qf * qf, axis=0, keepdims=True)
    kn = kn_ref[...]
    kn2 = jnp.where(pl.program_id(0) == 0, kn[0:1, :], kn[KN_ROWS:KN_ROWS + 1, :])
    bound = jnp.sqrt(qn2 * jnp.tile(kn2, (1, lanes // LANES))) * BOUND_SLACK
    single_pass = jnp.max(bound) <= SAFE_LOGIT_BOUND

    def kv_rows(j):
        return pl.ds(pl.multiple_of(j * tk, tk), tk)

    def finish():
        l_tot = jnp.sum(l_ref[...], axis=0, keepdims=True)
        out_t = acc_ref[...].reshape(HEAD_DIM, lanes) * (1.0 / l_tot)
        for t in range(n_tiles):
            for h in range(GROUP):
                lane0 = t * tile_lanes + h * tile_rows
                o_ref[t * tile_rows:(t + 1) * tile_rows, h * HEAD_DIM:(h + 1) * HEAD_DIM] = (
                    out_t[:, lane0:lane0 + tile_rows].T.astype(o_ref.dtype))

    l_ref[...] = jnp.zeros((SUBLANES, lanes), F32)
    acc_ref[...] = jnp.zeros(acc_shape, F32)

    @pl.when(single_pass)
    def _():
        m_ref[...] = jnp.broadcast_to(bound, (SUBLANES, lanes))

        def probs(j, slot):
            kt = k_ref[kv_rows(j), :]
            for lb in range(lanes // LANE_BLOCK):
                cols = slice(lb * LANE_BLOCK, (lb + 1) * LANE_BLOCK)
                s = jnp.dot(kt, qt_ref[:, cols], preferred_element_type=F32)
                p = jnp.exp2(s.reshape(n_groups, SUBLANES, LANE_BLOCK) - m_ref[:, cols])
                l_ref[:, cols] += jnp.sum(p, axis=0)
                p_buf[slot, :, cols] = p.reshape(tk, LANE_BLOCK).astype(BF16)

        def values(j, slot):
            pv = jnp.dot(vt_ref[:, kv_rows(j)], p_buf[slot], preferred_element_type=F32)
            acc_ref[...] += pv.reshape(acc_shape)

        probs(0, 0)

        def body(t, _):
            base = TILES_PER_TRIP * t
            for u in range(TILES_PER_TRIP):
                probs(base + u + 1, (u + 1) % 2)
                values(base + u, u % 2)
            return 0

        lax.fori_loop(0, (n_kv - 2) // TILES_PER_TRIP, body, 0)
        probs(n_kv - 1, 1)
        values(n_kv - 2, 0)
        values(n_kv - 1, 1)
        finish()

    @pl.when(jnp.logical_not(single_pass))
    def _():
        def scores(j, slot):
            s = jnp.dot(k_ref[kv_rows(j), :], qt_ref[...], preferred_element_type=F32)
            s = s.reshape(n_groups, SUBLANES, lanes)
            s_buf[slot] = s
            tmax_buf[slot] = jnp.max(s, axis=0)

        def softmax(slot):
            m_old = m_ref[...]
            m_new = jnp.maximum(m_old, _sublane_allmax(tmax_buf[slot]))
            alpha = jnp.exp2(m_old - m_new)
            psum = jnp.zeros((SUBLANES, lanes), F32)
            for c in range(n_groups // chunk):
                s = s_buf[slot, c * chunk:(c + 1) * chunk]
                p = jnp.exp2(s - m_new)
                psum = psum + jnp.sum(p, axis=0)
                p_buf[slot, c * SOFTMAX_ROWS:(c + 1) * SOFTMAX_ROWS, :] = (
                    p.reshape(SOFTMAX_ROWS, lanes).astype(BF16))
            l_ref[...] = alpha * l_ref[...] + psum
            m_ref[...] = m_new
            alpha_buf[slot] = alpha

        def values(j, slot):
            pv = jnp.dot(vt_ref[:, kv_rows(j)], p_buf[slot], preferred_element_type=F32)
            acc_ref[...] = alpha_buf[slot] * acc_ref[...] + pv.reshape(acc_shape)

        m_ref[...] = jnp.full((SUBLANES, lanes), NEG_BIG, F32)
        p_buf[1] = jnp.zeros((tk, lanes), BF16)
        alpha_buf[1] = jnp.ones((SUBLANES, lanes), F32)
        scores(0, 0)

        def body(t, _):
            for slot in range(2):
                cur = 2 * t + slot
                nxt = jnp.where(cur + 1 < n_kv, cur + 1, 0)
                prv = jnp.maximum(cur - 1, 0)
                scores(nxt, 1 - slot)
                softmax(slot)
                values(prv, 1 - slot)
            return 0

        lax.fori_loop(0, n_kv // 2, body, 0)
        values(n_kv - 1, 1)
        finish()


def _attention(k_norm2, q_t, k, v_t):
    S = k.shape[1]
    tk = ATTN_TK
    n_tiles = ATTN_Q_TILES
    tile_lanes = q_t.shape[-1]
    tq = n_tiles * (tile_lanes // GROUP)
    assert S % (2 * tk) == 0 and tk % SOFTMAX_ROWS == 0 and S % tq == 0
    assert TILES_PER_TRIP % 2 == 0 and (S // tk - 2) % TILES_PER_TRIP == 0
    lanes = n_tiles * tile_lanes
    est = (3 * HEAD_DIM * lanes * 2 + 2 * 2 * S * HEAD_DIM * 2 + 2 * tq * GROUP * HEAD_DIM * 2
           + (HEAD_DIM + 6 * SUBLANES) * lanes * 4 + 2 * tk * lanes * 6 + 2 * tk * lanes * 4)
    return pl.pallas_call(
        functools.partial(_attn_kernel, tk=tk),
        grid=(N_KV_HEADS, S // tq),
        in_specs=[
            pl.BlockSpec((N_KV_HEADS * KN_ROWS, LANES), lambda g, i: (0, 0)),
            pl.BlockSpec((None, n_tiles, HEAD_DIM, tile_lanes), lambda g, i: (g, i, 0, 0)),
            pl.BlockSpec((None, S, HEAD_DIM), lambda g, i: (g, 0, 0)),
            pl.BlockSpec((None, HEAD_DIM, S), lambda g, i: (g, 0, 0)),
        ],
        out_specs=pl.BlockSpec((tq, GROUP * HEAD_DIM), lambda g, i: (i, g)),
        out_shape=jax.ShapeDtypeStruct((S, ATTN_WIDTH), BF16),
        scratch_shapes=[
            pltpu.VMEM((HEAD_DIM, lanes), BF16),
            pltpu.VMEM((SUBLANES, lanes), F32),
            pltpu.VMEM((SUBLANES, lanes), F32),
            pltpu.VMEM((HEAD_DIM // SUBLANES, SUBLANES, lanes), F32),
            pltpu.VMEM((2, tk // SUBLANES, SUBLANES, lanes), F32),
            pltpu.VMEM((2, SUBLANES, lanes), F32),
            pltpu.VMEM((2, tk, lanes), BF16),
            pltpu.VMEM((2, SUBLANES, lanes), F32),
        ],
        compiler_params=pltpu.CompilerParams(
            dimension_semantics=("arbitrary", "arbitrary"), vmem_limit_bytes=_vmem_limit(est)),
        name="attention",
    )(k_norm2, q_t, k, v_t)


LRU_T = 512
SCAN_UNROLL = 16
assert LRU_WIDTH == SUBLANES * LANES and LRU_BLOCKS == SUBLANES


def _lru_gates(prev_ref, cur_ref, next_ref, ext_tm, xc_tm, a_tm, b_tm, has_prev, has_next,
               cw_ref, cb_ref, w_ref, bias_ref, lam_ref):
    rows = cur_ref.shape[0]
    T = rows // LRU_BLOCKS
    halo = prev_ref.shape[0]
    ext_tm[0:halo, :] = jnp.where(has_prev, prev_ref[...], 0.0)
    ext_tm[halo:halo + rows, :] = cur_ref[...]
    ext_tm[halo + rows:2 * halo + rows, :] = jnp.where(has_next, next_ref[...], 0.0)
    base = halo - CONV_LEFT * LRU_BLOCKS
    xh = cb_ref[...]
    for j in range(CONV_W):
        start = base + j * LRU_BLOCKS
        tap = ext_tm[start:start + rows, :].reshape(T, LRU_BLOCKS, LRU_BW)
        xh = xh + cw_ref[j] * tap
    xc_tm[...] = xh.reshape(rows, LRU_BW)
    lam = lam_ref[...]
    neg = -lam
    softplus = jnp.maximum(neg, 0.0) + jnp.log1p(jnp.exp(-jnp.abs(neg)))
    half_rate = (-0.5 * LRU_C * LOG2_E) * softplus
    for n in range(LRU_BLOCKS):
        cols = slice(n * LRU_BW, (n + 1) * LRU_BW)
        block_n = pl.ds(n, T, stride=LRU_BLOCKS)
        xb = xc_tm[block_n, :]
        z = jnp.dot(xb.astype(BF16), w_ref[n], preferred_element_type=F32)
        tanh_r = jnp.tanh(z[:, :LRU_BW] + bias_ref[0:1, cols])
        tanh_i = jnp.tanh(z[:, LRU_BW:] + bias_ref[1:2, cols])
        rate = half_rate[:, cols]
        a = jnp.exp2(tanh_r * rate + rate)
        one_minus_a2 = 1.0 - a * a
        mult = one_minus_a2 * lax.rsqrt(jnp.maximum(one_minus_a2, TINY))
        half_in = mult * xb
        a_tm[block_n, :] = a
        b_tm[block_n, :] = half_in * tanh_i + half_in


def _lru_scan(af_tm, bf_tm, ab_tm, bb_tm, hf_ref, hb_ref, carry_ref):
    T = af_tm.shape[0] // LRU_BLOCKS
    U = SCAN_UNROLL

    def store(out_ref, steps, t0):
        block = jnp.swapaxes(jnp.stack(steps), 0, 1)
        rows = pl.ds(pl.multiple_of(t0, U), U)
        for n in range(LRU_BLOCKS):
            out_ref[rows, n * LRU_BW:(n + 1) * LRU_BW] = block[n].astype(out_ref.dtype)

    def body(i, carry):
        h_f, h_b = carry
        fwd, bwd = [], []
        for u in range(U):
            t = i * U + u
            rf = pl.ds(pl.multiple_of(t * LRU_BLOCKS, LRU_BLOCKS), LRU_BLOCKS)
            rb = pl.ds(pl.multiple_of((T - 1 - t) * LRU_BLOCKS, LRU_BLOCKS), LRU_BLOCKS)
            h_f = af_tm[rf, :] * h_f + bf_tm[rf, :]
            h_b = ab_tm[rb, :] * h_b + bb_tm[rb, :]
            fwd.append(h_f)
            bwd.append(h_b)
        store(hf_ref, fwd, i * U)
        store(hb_ref, bwd[::-1], T - U - i * U)
        return h_f, h_b

    h_f, h_b = lax.fori_loop(0, T // U, body, (carry_ref[0], carry_ref[1]))
    carry_ref[0] = h_f
    carry_ref[1] = h_b


def _lru_kernel(fp_ref, fc_ref, fn_ref, bp_ref, bc_ref, bn_ref, cw_ref, cb_ref, w_ref, bias_ref,
                lam_ref, hf_ref, hb_ref, ext_tm, xc_tm, af_tm, bf_tm, ab_tm, bb_tm, carry_ref):
    c = pl.program_id(0)
    n_chunks = pl.num_programs(0)

    @pl.when(c == 0)
    def _():
        carry_ref[...] = jnp.zeros(carry_ref.shape, F32)

    _lru_gates(fp_ref, fc_ref, fn_ref, ext_tm, xc_tm, af_tm, bf_tm, c > 0, c < n_chunks - 1,
               cw_ref, cb_ref, w_ref.at[0], bias_ref.at[0], lam_ref.at[0])
    _lru_gates(bp_ref, bc_ref, bn_ref, ext_tm, xc_tm, ab_tm, bb_tm, c < n_chunks - 1, c > 0,
               cw_ref, cb_ref, w_ref.at[1], bias_ref.at[1], lam_ref.at[1])
    _lru_scan(af_tm, bf_tm, ab_tm, bb_tm, hf_ref, hb_ref, carry_ref)


LRU_HALO = SUBLANES


def _lru(xr_tm, conv_w_tm, conv_b_tm, w_gate, bias_gate, lam):
    S = xr_tm.shape[0] // LRU_BLOCKS
    T = LRU_T
    assert S % T == 0 and T % SCAN_UNROLL == 0 and T % LRU_HALO == 0
    assert CONV_LEFT <= LRU_HALO and CONV_W - 1 - CONV_LEFT <= LRU_HALO
    time_major = pltpu.VMEM((T * LRU_BLOCKS, LRU_BW), F32)
    n_chunks = S // T
    per = T // LRU_HALO
    last_blk = S // LRU_HALO - 1
    W = LRU_WIDTH
    chunk = (T * LRU_BLOCKS, LRU_BW)
    halo = (LRU_HALO * LRU_BLOCKS, LRU_BW)
    const2 = lambda c: (0, 0)
    est = (2 * 2 * (T * W * 4 + 2 * SUBLANES * W * 4) + 2 * 2 * T * W * 4
           + 2 * 2 * LRU_BLOCKS * LRU_BW * 2 * LRU_BW * 2 + (6 * T + 64) * W * 4 + 6 * T * W * 4)
    return pl.pallas_call(
        _lru_kernel,
        grid=(n_chunks,),
        in_specs=[
            pl.BlockSpec(halo, lambda c: (jnp.maximum(c * per - 1, 0), 0)),
            pl.BlockSpec(chunk, lambda c: (c, 0)),
            pl.BlockSpec(halo, lambda c: (jnp.minimum((c + 1) * per, last_blk), 0)),
            pl.BlockSpec(halo, lambda c: (jnp.maximum((n_chunks - 1 - c) * per - 1, 0), 0)),
            pl.BlockSpec(chunk, lambda c: (n_chunks - 1 - c, 0)),
            pl.BlockSpec(halo, lambda c: (jnp.minimum((n_chunks - c) * per, last_blk), 0)),
            pl.BlockSpec((CONV_W, LRU_BLOCKS, LRU_BW), lambda c: (0, 0, 0)),
            pl.BlockSpec((LRU_BLOCKS, LRU_BW), const2),
            pl.BlockSpec((2, LRU_BLOCKS, LRU_BW, 2 * LRU_BW), lambda c: (0, 0, 0, 0)),
            pl.BlockSpec((2, 2, W), lambda c: (0, 0, 0)),
            pl.BlockSpec((2, 1, W), lambda c: (0, 0, 0)),
        ],
        out_specs=[
            pl.BlockSpec((T, W), lambda c: (c, 0)),
            pl.BlockSpec((T, W), lambda c: (n_chunks - 1 - c, 0)),
        ],
        out_shape=[jax.ShapeDtypeStruct((S, W), BF16), jax.ShapeDtypeStruct((S, W), BF16)],
        scratch_shapes=[
            pltpu.VMEM(((T + 2 * LRU_HALO) * LRU_BLOCKS, LRU_BW), F32),
            time_major,
            time_major, time_major,
            time_major, time_major,
            pltpu.VMEM((2, SUBLANES, LRU_BW), F32),
        ],
        compiler_params=pltpu.CompilerParams(
            dimension_semantics=("arbitrary",), vmem_limit_bytes=_vmem_limit(est)),
        name="rg_lru",
    )(xr_tm, xr_tm, xr_tm, xr_tm, xr_tm, xr_tm, conv_w_tm, conv_b_tm, w_gate, bias_gate, lam)


OUT_TM = 512


def _out_proj_kernel(ao_ref, ga_ref, hf_ref, hb_ref, gl_ref, x_ref, anw_ref, lnw_ref, w_ref,
                     o_ref):
    def inv_rms(v):
        return lax.rsqrt(jnp.mean(v * v, axis=-1, keepdims=True) + EPS)

    ao = ao_ref[...].astype(F32)
    hl = hf_ref[...].astype(F32) + hb_ref[...].astype(F32)
    ya = (ao * anw_ref[...] * ga_ref[...].astype(F32)).astype(BF16)
    yl = (hl * lnw_ref[...] * gl_ref[...].astype(F32)).astype(BF16)
    mix_a = jnp.dot(ya, w_ref[0:ATTN_WIDTH, :], preferred_element_type=F32) * inv_rms(ao)
    mix_l = jnp.dot(yl, w_ref[ATTN_WIDTH:MIX_WIDTH, :], preferred_element_type=F32) * inv_rms(hl)
    o_ref[...] = x_ref[...] + mix_a + mix_l


def _out_proj(attn_out, g_attn, h_fwd, h_bwd, g_lru, x2, attn_norm_w, lru_norm_w, w_out_bf16):
    S = x2.shape[0]
    tm = OUT_TM
    row = lambda i: (i, 0)
    const = lambda i: (0, 0)
    est = (2 * 5 * tm * ATTN_WIDTH * 2 + 2 * 2 * tm * D_MODEL * 4 + MIX_WIDTH * D_MODEL * 2
           + 4 * tm * D_MODEL * 4)
    return pl.pallas_call(
        _out_proj_kernel,
        grid=(S // tm,),
        in_specs=[
            pl.BlockSpec((tm, ATTN_WIDTH), row),
            pl.BlockSpec((tm, ATTN_WIDTH), row),
            pl.BlockSpec((tm, LRU_WIDTH), row),
            pl.BlockSpec((tm, LRU_WIDTH), row),
            pl.BlockSpec((tm, LRU_WIDTH), row),
            pl.BlockSpec((tm, D_MODEL), row),
            pl.BlockSpec((1, ATTN_WIDTH), const),
            pl.BlockSpec((1, LRU_WIDTH), const),
            pl.BlockSpec((MIX_WIDTH, D_MODEL), const, pipeline_mode=pl.Buffered(1)),
        ],
        out_specs=pl.BlockSpec((tm, D_MODEL), row),
        out_shape=jax.ShapeDtypeStruct((S, D_MODEL), F32),
        compiler_params=pltpu.CompilerParams(
            dimension_semantics=("arbitrary",), vmem_limit_bytes=_vmem_limit(est)),
        name="out_proj",
    )(attn_out, g_attn, h_fwd, h_bwd, g_lru, x2, attn_norm_w, lru_norm_w, w_out_bf16)


def _rope_tables(seq_len, tile_rows):
    rows = seq_len // GRID_W
    inv_freq = ROPE_THETA ** (-jnp.arange(ROPE_PAIRS, dtype=F32) / ROPE_PAIRS)
    ang_r = jnp.arange(rows, dtype=F32)[:, None] * inv_freq[None, :]
    ang_c = jnp.arange(GRID_W, dtype=F32)[:, None] * inv_freq[None, :]
    cr, sr, cc, sc = jnp.cos(ang_r), jnp.sin(ang_r), jnp.cos(ang_c), jnp.sin(ang_c)
    zr, zc = jnp.zeros_like(cr), jnp.zeros_like(cc)
    reps = tile_rows // GRID_W
    return (jnp.concatenate([cr, cr, zr, zr], axis=-1),
            jnp.concatenate([-sr, sr, zr, zr], axis=-1),
            jnp.tile(jnp.concatenate([zc, zc, cc, cc], axis=-1), (reps, 1)),
            jnp.tile(jnp.concatenate([zc, zc, -sc, sc], axis=-1), (reps, 1)))


def _layer(x2, norm_w, w_in, q_norm_w, k_norm_w, conv_w, conv_b, lru_wa, lru_ba, lru_wx, lru_bx,
           lru_lambda, attn_norm_w, lru_norm_w, w_out):
    S = x2.shape[0]
    q_t, k, v_t, k_norm2, g_attn, xr, g_lru = _in_proj(
        x2, norm_w.reshape(1, D_MODEL), w_in.astype(BF16), q_norm_w.reshape(1, HEAD_DIM),
        k_norm_w.reshape(1, HEAD_DIM), *_rope_tables(S, PROJ_TM))
    attn_out = _attention(k_norm2, q_t, k, v_t)
    w_gate = jnp.concatenate([lru_wa, lru_wx], axis=-1).astype(BF16)
    bias_gate = 0.5 * jnp.stack([lru_ba, lru_bx], axis=1)
    h_fwd, h_bwd = _lru(xr, 0.5 * conv_w.reshape(CONV_W, LRU_BLOCKS, LRU_BW),
                        0.5 * conv_b.reshape(LRU_BLOCKS, LRU_BW),
                        w_gate, bias_gate, lru_lambda.reshape(2, 1, LRU_WIDTH))
    return _out_proj(attn_out, g_attn, h_fwd, h_bwd, g_lru, x2,
                     attn_norm_w.reshape(1, ATTN_WIDTH), lru_norm_w.reshape(1, LRU_WIDTH),
                     w_out.astype(BF16))


def kernel(x, norm_w, w_in, q_norm_w, k_norm_w, conv_w, conv_b, lru_wa, lru_ba, lru_wx, lru_bx,
           lru_lambda, attn_norm_w, lru_norm_w, w_out):
    B, S, D = x.shape
    assert D == D_MODEL and S % GRID_W == 0
    outs = [_layer(x[b], norm_w, w_in, q_norm_w, k_norm_w, conv_w, conv_b, lru_wa, lru_ba,
                   lru_wx, lru_bx, lru_lambda, attn_norm_w, lru_norm_w, w_out)
            for b in range(B)]
    return jnp.stack(outs, axis=0) if B > 1 else outs[0][None]
```

```python
import functools
import math

import jax
import jax.numpy as jnp
from jax import lax
from jax.experimental import pallas as pl
from jax.experimental.pallas import tpu as pltpu

F32 = jnp.float32
BF16 = jnp.bfloat16

D_MODEL = 2048
N_HEADS = 8
N_KV_HEADS = 2
GROUP = N_HEADS // N_KV_HEADS
HEAD_DIM = 128
ATTN_WIDTH = N_HEADS * HEAD_DIM
KV_WIDTH = N_KV_HEADS * HEAD_DIM
ROPE_THETA = 10000.0
ROPE_PAIRS = HEAD_DIM // 4
GRID_W = 64
LRU_WIDTH = D_MODEL // 2
LRU_BLOCKS = 8
LRU_BW = LRU_WIDTH // LRU_BLOCKS
LRU_C = 8.0
CONV_W = 4
CONV_LEFT = 2
MIX_WIDTH = ATTN_WIDTH + LRU_WIDTH
IN_WIDTH = 2 * ATTN_WIDTH + 2 * KV_WIDTH + 2 * LRU_WIDTH
EPS = 1e-6

COL_Q = 0
COL_K = ATTN_WIDTH
COL_V = ATTN_WIDTH + KV_WIDTH
COL_GA = ATTN_WIDTH + 2 * KV_WIDTH
COL_XR = COL_GA + ATTN_WIDTH
COL_GL = COL_XR + LRU_WIDTH

SUBLANES = 8
LANES = 128
VMEM_LIMIT_CAP = 60000 * 1024
MIB = 1024 * 1024

NEG_BIG = -1e30
TINY = 1e-30
LOG2_E = math.log2(math.e)


def _vmem_limit(estimate_bytes):
    return int(min(VMEM_LIMIT_CAP, estimate_bytes + 8 * MIB))


def _sigmoid(x):
    return 0.5 * jnp.tanh(0.5 * x) + 0.5


PROJ_TM = 256
PROJ_TN = 512
KN_ROWS = SUBLANES // N_KV_HEADS


def _in_proj_kernel(x_ref, nw_ref, w_ref, qw_ref, kw_ref, rcos_ref, rsin_ref, ccos_ref, csin_ref,
                    q_ref, k_ref, v_ref, kn_ref, ga_ref, xr_ref, gl_ref):
    @pl.when(pl.program_id(0) == 0)
    def _():
        kn_ref[...] = jnp.zeros(kn_ref.shape, F32)

    x = x_ref[...]
    ms = jnp.mean(x * x, axis=-1, keepdims=True)
    inv_rms = lax.rsqrt(ms + EPS)
    h = (x * nw_ref[...]).astype(BF16)

    def rope_table(row_ref, col_ref):
        grid_rows = x_ref.shape[0] // GRID_W
        first = pl.program_id(0) * grid_rows
        lines = [jnp.broadcast_to(row_ref[pl.ds(first + r, 1), :], (GRID_W, HEAD_DIM))
                 for r in range(grid_rows)]
        return jnp.concatenate(lines, axis=0) + col_ref[...]

    cos = rope_table(rcos_ref, ccos_ref)
    sin = rope_table(rsin_ref, csin_ref)
    lane = lax.broadcasted_iota(jnp.int32, (1, HEAD_DIM), 1)
    low_half = (lane % (2 * ROPE_PAIRS)) < ROPE_PAIRS

    def proj(col, width):
        return jnp.dot(h, w_ref[:, col:col + width], preferred_element_type=F32) * inv_rms

    def norm_rope(seg, gain, scale):
        ms_h = jnp.mean(seg * seg, axis=-1, keepdims=True)
        y = seg * lax.rsqrt(ms_h + EPS) * gain
        partner = jnp.where(low_half, pltpu.roll(y, HEAD_DIM - ROPE_PAIRS, 1),
                            pltpu.roll(y, ROPE_PAIRS, 1))
        return (y * cos + partner * sin) * scale

    scale = HEAD_DIM ** -0.5 * LOG2_E
    qw = qw_ref[...]
    kw = kw_ref[...]
    tm = x_ref.shape[0]
    for cb in range(ATTN_WIDTH // PROJ_TN):
        acc = proj(COL_Q + cb * PROJ_TN, PROJ_TN)
        for hh in range(PROJ_TN // HEAD_DIM):
            head = cb * (PROJ_TN // HEAD_DIM) + hh
            seg = acc[:, hh * HEAD_DIM:(hh + 1) * HEAD_DIM]
            qt = norm_rope(seg, qw, scale).T.astype(BF16)
            lane0 = (head % GROUP) * tm
            q_ref[head // GROUP, 0, :, lane0:lane0 + tm] = qt
    acc = proj(COL_K, 2 * KV_WIDTH)
    for hh in range(N_KV_HEADS):
        kb = norm_rope(acc[:, hh * HEAD_DIM:(hh + 1) * HEAD_DIM], kw, 1.0).astype(BF16)
        k_ref[hh] = kb
        kf = kb.astype(F32)
        n2 = jnp.max(jnp.sum(kf * kf, axis=-1, keepdims=True), axis=0, keepdims=True)
        rows = slice(hh * KN_ROWS, (hh + 1) * KN_ROWS)
        kn_ref[rows, :] = jnp.maximum(kn_ref[rows, :], n2)
        vseg = acc[:, KV_WIDTH + hh * HEAD_DIM:KV_WIDTH + (hh + 1) * HEAD_DIM]
        v_ref[hh] = vseg.T.astype(BF16)
    for cb in range(ATTN_WIDTH // PROJ_TN):
        g = proj(COL_GA + cb * PROJ_TN, PROJ_TN)
        ga_ref[:, cb * PROJ_TN:(cb + 1) * PROJ_TN] = (g * _sigmoid(g)).astype(BF16)
    for cb in range(LRU_WIDTH // PROJ_TN):
        g = proj(COL_GL + cb * PROJ_TN, PROJ_TN)
        gl_ref[:, cb * PROJ_TN:(cb + 1) * PROJ_TN] = (g * _sigmoid(g)).astype(BF16)
    for cb in range(LRU_WIDTH // PROJ_TN):
        acc = proj(COL_XR + cb * PROJ_TN, PROJ_TN)
        for q in range(PROJ_TN // LANES):
            n = cb * (PROJ_TN // LANES) + q
            xr_ref[pl.ds(n, tm, stride=LRU_BLOCKS), :] = acc[:, q * LANES:(q + 1) * LANES]


def _in_proj(x2, norm_w, w_in_bf16, q_norm_w, k_norm_w, row_cos, row_sin, col_cos, col_sin):
    S = x2.shape[0]
    tm = PROJ_TM
    assert tm % GRID_W == 0 and S % tm == 0
    const = lambda i: (0, 0)
    est = (2 * tm * D_MODEL * 4 + D_MODEL * IN_WIDTH * 2 + 4 * tm * HEAD_DIM * 4
           + 2 * (3 * tm * ATTN_WIDTH * 2 + 2 * tm * KV_WIDTH * 2 + tm * LRU_WIDTH * 4)
           + tm * D_MODEL * 6 + 4 * tm * PROJ_TN * 4)
    return pl.pallas_call(
        _in_proj_kernel,
        grid=(S // tm,),
        in_specs=[
            pl.BlockSpec((tm, D_MODEL), lambda i: (i, 0)),
            pl.BlockSpec((1, D_MODEL), const),
            pl.BlockSpec((D_MODEL, IN_WIDTH), const, pipeline_mode=pl.Buffered(1)),
            pl.BlockSpec((1, HEAD_DIM), const),
            pl.BlockSpec((1, HEAD_DIM), const),
            pl.BlockSpec((S // GRID_W, HEAD_DIM), const),
            pl.BlockSpec((S // GRID_W, HEAD_DIM), const),
            pl.BlockSpec((tm, HEAD_DIM), const),
            pl.BlockSpec((tm, HEAD_DIM), const),
        ],
        out_specs=[
            pl.BlockSpec((N_KV_HEADS, 1, HEAD_DIM, GROUP * tm), lambda i: (0, i, 0, 0)),
            pl.BlockSpec((N_KV_HEADS, tm, HEAD_DIM), lambda i: (0, i, 0)),
            pl.BlockSpec((N_KV_HEADS, HEAD_DIM, tm), lambda i: (0, 0, i)),
            pl.BlockSpec((N_KV_HEADS * KN_ROWS, LANES), const),
            pl.BlockSpec((tm, ATTN_WIDTH), lambda i: (i, 0)),
            pl.BlockSpec((tm * LRU_BLOCKS, LANES), lambda i: (i, 0)),
            pl.BlockSpec((tm, LRU_WIDTH), lambda i: (i, 0)),
        ],
        out_shape=[
            jax.ShapeDtypeStruct((N_KV_HEADS, S // tm, HEAD_DIM, GROUP * tm), BF16),
            jax.ShapeDtypeStruct((N_KV_HEADS, S, HEAD_DIM), BF16),
            jax.ShapeDtypeStruct((N_KV_HEADS, HEAD_DIM, S), BF16),
            jax.ShapeDtypeStruct((N_KV_HEADS * KN_ROWS, LANES), F32),
            jax.ShapeDtypeStruct((S, ATTN_WIDTH), BF16),
            jax.ShapeDtypeStruct((S * LRU_BLOCKS, LANES), F32),
            jax.ShapeDtypeStruct((S, LRU_WIDTH), BF16),
        ],
        compiler_params=pltpu.CompilerParams(
            dimension_semantics=("arbitrary",), vmem_limit_bytes=_vmem_limit(est)),
        name="in_proj",
    )(x2, norm_w, w_in_bf16, q_norm_w, k_norm_w, row_cos, row_sin, col_cos, col_sin)


ATTN_Q_TILES = 2
ATTN_TK = 1024
TILES_PER_TRIP = 2


SOFTMAX_ROWS = 32


def _sublane_allmax(x):
    for shift in (4, 2, 1):
        x = jnp.maximum(x, pltpu.roll(x, shift, 0))
    return x


SAFE_LOGIT_BOUND = 55.0
BOUND_SLACK = 1.0 + 2.0 ** -10
LANE_BLOCK = 256


def _attn_kernel(kn_ref, q_tiles_ref, k_ref, vt_ref, o_ref, qt_ref, m_ref, l_ref, acc_ref, s_buf,
                 tmax_buf, p_buf, alpha_buf, *, tk):
    n_tiles, _, tile_lanes = q_tiles_ref.shape
    tile_rows = tile_lanes // GROUP
    lanes = n_tiles * tile_lanes
    n_kv = k_ref.shape[0] // tk
    n_groups = tk // SUBLANES
    chunk = SOFTMAX_ROWS // SUBLANES
    acc_shape = (HEAD_DIM // SUBLANES, SUBLANES, lanes)

    for t in range(n_tiles):
        qt_ref[:, t * tile_lanes:(t + 1) * tile_lanes] = q_tiles_ref[t]
    qf = qt_ref[...].astype(F32)
    qn2 = jnp.sum(qf * qf, axis=0, keepdims=True)
    kn = kn_ref[...]
    kn2 = jnp.where(pl.program_id(0) == 0, kn[0:1, :], kn[KN_ROWS:KN_ROWS + 1, :])
    bound = jnp.sqrt(qn2 * jnp.tile(kn2, (1, lanes // LANES))) * BOUND_SLACK
    single_pass = jnp.max(bound) <= SAFE_LOGIT_BOUND

    def kv_rows(j):
        return pl.ds(pl.multiple_of(j * tk, tk), tk)

    def finish():
        l_tot = jnp.sum(l_ref[...], axis=0, keepdims=True)
        out_t = acc_ref[...].reshape(HEAD_DIM, lanes) * (1.0 / l_tot)
        for t in range(n_tiles):
            for h in range(GROUP):
                lane0 = t * tile_lanes + h * tile_rows
                o_ref[t * tile_rows:(t + 1) * tile_rows, h * HEAD_DIM:(h + 1) * HEAD_DIM] = (
                    out_t[:, lane0:lane0 + tile_rows].T.astype(o_ref.dtype))

    l_ref[...] = jnp.zeros((SUBLANES, lanes), F32)
    acc_ref[...] = jnp.zeros(acc_shape, F32)

    @pl.when(single_pass)
    def _():
        m_ref[...] = jnp.broadcast_to(bound, (SUBLANES, lanes))

        def probs(j, slot):
            kt = k_ref[kv_rows(j), :]
            for lb in range(lanes // LANE_BLOCK):
                cols = slice(lb * LANE_BLOCK, (lb + 1) * LANE_BLOCK)
                s = jnp.dot(kt, qt_ref[:, cols], preferred_element_type=F32)
                p = jnp.exp2(s.reshape(n_groups, SUBLANES, LANE_BLOCK) - m_ref[:, cols])
                l_ref[:, cols] += jnp.sum(p, axis=0)
                p_buf[slot, :, cols] = p.reshape(tk, LANE_BLOCK).astype(BF16)

        def values(j, slot):
            pv = jnp.dot(vt_ref[:, kv_rows(j)], p_buf[slot], preferred_element_type=F32)
            acc_ref[...] += pv.reshape(acc_shape)

        probs(0, 0)

        def body(t, _):
            base = TILES_PER_TRIP * t
            for u in range(TILES_PER_TRIP):
                probs(base + u + 1, (u + 1) % 2)
                values(base + u, u % 2)
            return 0

        lax.fori_loop(0, (n_kv - 2) // TILES_PER_TRIP, body, 0)
        probs(n_kv - 1, 1)
        values(n_kv - 2, 0)
        values(n_kv - 1, 1)
        finish()

    @pl.when(jnp.logical_not(single_pass))
    def _():
        def scores(j, slot):
            s = jnp.dot(k_ref[kv_rows(j), :], qt_ref[...], preferred_element_type=F32)
            s = s.reshape(n_groups, SUBLANES, lanes)
            s_buf[slot] = s
            tmax_buf[slot] = jnp.max(s, axis=0)

        def softmax(slot):
            m_old = m_ref[...]
            m_new = jnp.maximum(m_old, _sublane_allmax(tmax_buf[slot]))
            alpha = jnp.exp2(m_old - m_new)
            psum = jnp.zeros((SUBLANES, lanes), F32)
            for c in range(n_groups // chunk):
                s = s_buf[slot, c * chunk:(c + 1) * chunk]
                p = jnp.exp2(s - m_new)
                psum = psum + jnp.sum(p, axis=0)
                p_buf[slot, c * SOFTMAX_ROWS:(c + 1) * SOFTMAX_ROWS, :] = (
                    p.reshape(SOFTMAX_ROWS, lanes).astype(BF16))
            l_ref[...] = alpha * l_ref[...] + psum
            m_ref[...] = m_new
            alpha_buf[slot] = alpha

        def values(j, slot):
            pv = jnp.dot(vt_ref[:, kv_rows(j)], p_buf[slot], preferred_element_type=F32)
            acc_ref[...] = alpha_buf[slot] * acc_ref[...] + pv.reshape(acc_shape)

        m_ref[...] = jnp.full((SUBLANES, lanes), NEG_BIG, F32)
        p_buf[1] = jnp.zeros((tk, lanes), BF16)
        alpha_buf[1] = jnp.ones((SUBLANES, lanes), F32)
        scores(0, 0)

        def body(t, _):
            for slot in range(2):
                cur = 2 * t + slot
                nxt = jnp.where(cur + 1 < n_kv, cur + 1, 0)
                prv = jnp.maximum(cur - 1, 0)
                scores(nxt, 1 - slot)
                softmax(slot)
                values(prv, 1 - slot)
            return 0

        lax.fori_loop(0, n_kv // 2, body, 0)
        values(n_kv - 1, 1)
        finish()


def _attention(k_norm2, q_t, k, v_t):
    S = k.shape[1]
    tk = ATTN_TK
    n_tiles = ATTN_Q_TILES
    tile_lanes = q_t.shape[-1]
    tq = n_tiles * (tile_lanes // GROUP)
    assert S % (2 * tk) == 0 and tk % SOFTMAX_ROWS == 0 and S % tq == 0
    assert TILES_PER_TRIP % 2 == 0 and (S // tk - 2) % TILES_PER_TRIP == 0
    lanes = n_tiles * tile_lanes
    est = (3 * HEAD_DIM * lanes * 2 + 2 * 2 * S * HEAD_DIM * 2 + 2 * tq * GROUP * HEAD_DIM * 2
           + (HEAD_DIM + 6 * SUBLANES) * lanes * 4 + 2 * tk * lanes * 6 + 2 * tk * lanes * 4)
    return pl.pallas_call(
        functools.partial(_attn_kernel, tk=tk),
        grid=(N_KV_HEADS, S // tq),
        in_specs=[
            pl.BlockSpec((N_KV_HEADS * KN_ROWS, LANES), lambda g, i: (0, 0)),
            pl.BlockSpec((None, n_tiles, HEAD_DIM, tile_lanes), lambda g, i: (g, i, 0, 0)),
            pl.BlockSpec((None, S, HEAD_DIM), lambda g, i: (g, 0, 0)),
            pl.BlockSpec((None, HEAD_DIM, S), lambda g, i: (g, 0, 0)),
        ],
        out_specs=pl.BlockSpec((tq, GROUP * HEAD_DIM), lambda g, i: (i, g)),
        out_shape=jax.ShapeDtypeStruct((S, ATTN_WIDTH), BF16),
        scratch_shapes=[
            pltpu.VMEM((HEAD_DIM, lanes), BF16),
            pltpu.VMEM((SUBLANES, lanes), F32),
            pltpu.VMEM((SUBLANES, lanes), F32),
            pltpu.VMEM((HEAD_DIM // SUBLANES, SUBLANES, lanes), F32),
            pltpu.VMEM((2, tk // SUBLANES, SUBLANES, lanes), F32),
            pltpu.VMEM((2, SUBLANES, lanes), F32),
            pltpu.VMEM((2, tk, lanes), BF16),
            pltpu.VMEM((2, SUBLANES, lanes), F32),
        ],
        compiler_params=pltpu.CompilerParams(
            dimension_semantics=("arbitrary", "arbitrary"), vmem_limit_bytes=_vmem_limit(est)),
        name="attention",
    )(k_norm2, q_t, k, v_t)


LRU_T = 512
SCAN_UNROLL = 16
assert LRU_WIDTH == SUBLANES * LANES and LRU_BLOCKS == SUBLANES


def _lru_gates(prev_ref, cur_ref, next_ref, ext_tm, xc_tm, a_tm, b_tm, has_prev, has_next,
               cw_ref, cb_ref, w_ref, bias_ref, lam_ref):
    rows = cur_ref.shape[0]
    T = rows // LRU_BLOCKS
    halo = prev_ref.shape[0]
    ext_tm[0:halo, :] = jnp.where(has_prev, prev_ref[...], 0.0)
    ext_tm[halo:halo + rows, :] = cur_ref[...]
    ext_tm[halo + rows:2 * halo + rows, :] = jnp.where(has_next, next_ref[...], 0.0)
    base = halo - CONV_LEFT * LRU_BLOCKS
    xh = cb_ref[...]
    for j in range(CONV_W):
        start = base + j * LRU_BLOCKS
        tap = ext_tm[start:start + rows, :].reshape(T, LRU_BLOCKS, LRU_BW)
        xh = xh + cw_ref[j] * tap
    xc_tm[...] = xh.reshape(rows, LRU_BW)
    lam = lam_ref[...]
    neg = -lam
    softplus = jnp.maximum(neg, 0.0) + jnp.log1p(jnp.exp(-jnp.abs(neg)))
    half_rate = (-0.5 * LRU_C * LOG2_E) * softplus
    for n in range(LRU_BLOCKS):
        cols = slice(n * LRU_BW, (n + 1) * LRU_BW)
        block_n = pl.ds(n, T, stride=LRU_BLOCKS)
        xb = xc_tm[block_n, :]
        z = jnp.dot(xb.astype(BF16), w_ref[n], preferred_element_type=F32)
        tanh_r = jnp.tanh(z[:, :LRU_BW] + bias_ref[0:1, cols])
        tanh_i = jnp.tanh(z[:, LRU_BW:] + bias_ref[1:2, cols])
        rate = half_rate[:, cols]
        a = jnp.exp2(tanh_r * rate + rate)
        one_minus_a2 = 1.0 - a * a
        mult = one_minus_a2 * lax.rsqrt(jnp.maximum(one_minus_a2, TINY))
        half_in = mult * xb
        a_tm[block_n, :] = a
        b_tm[block_n, :] = half_in * tanh_i + half_in


def _lru_scan(af_tm, bf_tm, ab_tm, bb_tm, hf_ref, hb_ref, tiles_f, tiles_b, carry_ref):
    T = af_tm.shape[0] // LRU_BLOCKS
    U = SCAN_UNROLL
    tile_rows = LRU_BLOCKS * SUBLANES

    def step_rows(base_t, off):
        return pl.ds(pl.multiple_of(base_t * LRU_BLOCKS, U * LRU_BLOCKS) + off * LRU_BLOCKS,
                     LRU_BLOCKS)

    def scatter(tiles, base_t, off, h):
        start = (pl.multiple_of(base_t * LRU_BLOCKS, U * LRU_BLOCKS)
                 + (off // SUBLANES) * tile_rows + off % SUBLANES)
        tiles[pl.ds(start, LRU_BLOCKS, stride=SUBLANES), :] = h

    def pair(a_tm, b_tm, tiles, base_t, off0, off1, h):
        a0, b0 = a_tm[step_rows(base_t, off0), :], b_tm[step_rows(base_t, off0), :]
        a1, b1 = a_tm[step_rows(base_t, off1), :], b_tm[step_rows(base_t, off1), :]
        h0 = a0 * h + b0
        h1 = (a1 * a0) * h + (a1 * b0 + b1)
        scatter(tiles, base_t, off0, h0)
        scatter(tiles, base_t, off1, h1)
        return h1

    def store(out_ref, tiles, base_t):
        grp = tiles[pl.ds(pl.multiple_of(base_t * LRU_BLOCKS, U * LRU_BLOCKS), U * LRU_BLOCKS), :]
        grp = grp.reshape(U // SUBLANES, LRU_BLOCKS, SUBLANES, LRU_BW)
        rows = pl.ds(pl.multiple_of(base_t, U), U)
        for n in range(LRU_BLOCKS):
            out_ref[rows, n * LRU_BW:(n + 1) * LRU_BW] = (
                grp[:, n].reshape(U, LRU_BW).astype(out_ref.dtype))

    def body(i, carry):
        h_f, h_b = carry
        base_f = i * U
        base_b = T - U - i * U
        for u in range(0, U, 2):
            h_f = pair(af_tm, bf_tm, tiles_f, base_f, u, u + 1, h_f)
            h_b = pair(ab_tm, bb_tm, tiles_b, base_b, U - 1 - u, U - 2 - u, h_b)
        store(hf_ref, tiles_f, base_f)
        store(hb_ref, tiles_b, base_b)
        return h_f, h_b

    h_f, h_b = lax.fori_loop(0, T // U, body, (carry_ref[0], carry_ref[1]))
    carry_ref[0] = h_f
    carry_ref[1] = h_b


def _lru_kernel(fp_ref, fc_ref, fn_ref, bp_ref, bc_ref, bn_ref, cw_ref, cb_ref, w_ref, bias_ref,
                lam_ref, hf_ref, hb_ref, ext_tm, xc_tm, af_tm, bf_tm, ab_tm, bb_tm, tiles_f,
                tiles_b, carry_ref):
    c = pl.program_id(0)
    n_chunks = pl.num_programs(0)

    @pl.when(c == 0)
    def _():
        carry_ref[...] = jnp.zeros(carry_ref.shape, F32)

    _lru_gates(fp_ref, fc_ref, fn_ref, ext_tm, xc_tm, af_tm, bf_tm, c > 0, c < n_chunks - 1,
               cw_ref, cb_ref, w_ref.at[0], bias_ref.at[0], lam_ref.at[0])
    _lru_gates(bp_ref, bc_ref, bn_ref, ext_tm, xc_tm, ab_tm, bb_tm, c < n_chunks - 1, c > 0,
               cw_ref, cb_ref, w_ref.at[1], bias_ref.at[1], lam_ref.at[1])
    _lru_scan(af_tm, bf_tm, ab_tm, bb_tm, hf_ref, hb_ref, tiles_f, tiles_b, carry_ref)


LRU_HALO = SUBLANES


def _lru(xr_tm, conv_w_tm, conv_b_tm, w_gate, bias_gate, lam):
    S = xr_tm.shape[0] // LRU_BLOCKS
    T = LRU_T
    assert S % T == 0 and T % SCAN_UNROLL == 0 and T % LRU_HALO == 0
    assert CONV_LEFT <= LRU_HALO and CONV_W - 1 - CONV_LEFT <= LRU_HALO
    time_major = pltpu.VMEM((T * LRU_BLOCKS, LRU_BW), F32)
    n_chunks = S // T
    per = T // LRU_HALO
    last_blk = S // LRU_HALO - 1
    W = LRU_WIDTH
    chunk = (T * LRU_BLOCKS, LRU_BW)
    halo = (LRU_HALO * LRU_BLOCKS, LRU_BW)
    const2 = lambda c: (0, 0)
    est = (2 * 2 * (T * W * 4 + 2 * SUBLANES * W * 4) + 2 * 2 * T * W * 4
           + 2 * 2 * LRU_BLOCKS * LRU_BW * 2 * LRU_BW * 2 + (8 * T + 64) * W * 4 + 6 * T * W * 4)
    return pl.pallas_call(
        _lru_kernel,
        grid=(n_chunks,),
        in_specs=[
            pl.BlockSpec(halo, lambda c: (jnp.maximum(c * per - 1, 0), 0)),
            pl.BlockSpec(chunk, lambda c: (c, 0)),
            pl.BlockSpec(halo, lambda c: (jnp.minimum((c + 1) * per, last_blk), 0)),
            pl.BlockSpec(halo, lambda c: (jnp.maximum((n_chunks - 1 - c) * per - 1, 0), 0)),
            pl.BlockSpec(chunk, lambda c: (n_chunks - 1 - c, 0)),
            pl.BlockSpec(halo, lambda c: (jnp.minimum((n_chunks - c) * per, last_blk), 0)),
            pl.BlockSpec((CONV_W, LRU_BLOCKS, LRU_BW), lambda c: (0, 0, 0)),
            pl.BlockSpec((LRU_BLOCKS, LRU_BW), const2),
            pl.BlockSpec((2, LRU_BLOCKS, LRU_BW, 2 * LRU_BW), lambda c: (0, 0, 0, 0)),
            pl.BlockSpec((2, 2, W), lambda c: (0, 0, 0)),
            pl.BlockSpec((2, 1, W), lambda c: (0, 0, 0)),
        ],
        out_specs=[
            pl.BlockSpec((T, W), lambda c: (c, 0)),
            pl.BlockSpec((T, W), lambda c: (n_chunks - 1 - c, 0)),
        ],
        out_shape=[jax.ShapeDtypeStruct((S, W), BF16), jax.ShapeDtypeStruct((S, W), BF16)],
        scratch_shapes=[
            pltpu.VMEM(((T + 2 * LRU_HALO) * LRU_BLOCKS, LRU_BW), F32),
            time_major,
            time_major, time_major,
            time_major, time_major,
            time_major, time_major,
            pltpu.VMEM((2, SUBLANES, LRU_BW), F32),
        ],
        compiler_params=pltpu.CompilerParams(
            dimension_semantics=("arbitrary",), vmem_limit_bytes=_vmem_limit(est)),
        name="rg_lru",
    )(xr_tm, xr_tm, xr_tm, xr_tm, xr_tm, xr_tm, conv_w_tm, conv_b_tm, w_gate, bias_gate, lam)


OUT_TM = 512


def _out_proj_kernel(ao_ref, ga_ref, hf_ref, hb_ref, gl_ref, x_ref, anw_ref, lnw_ref, w_ref,
                     o_ref):
    def inv_rms(v):
        return lax.rsqrt(jnp.mean(v * v, axis=-1, keepdims=True) + EPS)

    ao = ao_ref[...].astype(F32)
    hl = hf_ref[...].astype(F32) + hb_ref[...].astype(F32)
    ya = (ao * anw_ref[...] * ga_ref[...].astype(F32)).astype(BF16)
    yl = (hl * lnw_ref[...] * gl_ref[...].astype(F32)).astype(BF16)
    mix_a = jnp.dot(ya, w_ref[0:ATTN_WIDTH, :], preferred_element_type=F32) * inv_rms(ao)
    mix_l = jnp.dot(yl, w_ref[ATTN_WIDTH:MIX_WIDTH, :], preferred_element_type=F32) * inv_rms(hl)
    o_ref[...] = x_ref[...] + mix_a + mix_l


def _out_proj(attn_out, g_attn, h_fwd, h_bwd, g_lru, x2, attn_norm_w, lru_norm_w, w_out_bf16):
    S = x2.shape[0]
    tm = OUT_TM
    row = lambda i: (i, 0)
    const = lambda i: (0, 0)
    est = (2 * 5 * tm * ATTN_WIDTH * 2 + 2 * 2 * tm * D_MODEL * 4 + MIX_WIDTH * D_MODEL * 2
           + 4 * tm * D_MODEL * 4)
    return pl.pallas_call(
        _out_proj_kernel,
        grid=(S // tm,),
        in_specs=[
            pl.BlockSpec((tm, ATTN_WIDTH), row),
            pl.BlockSpec((tm, ATTN_WIDTH), row),
            pl.BlockSpec((tm, LRU_WIDTH), row),
            pl.BlockSpec((tm, LRU_WIDTH), row),
            pl.BlockSpec((tm, LRU_WIDTH), row),
            pl.BlockSpec((tm, D_MODEL), row),
            pl.BlockSpec((1, ATTN_WIDTH), const),
            pl.BlockSpec((1, LRU_WIDTH), const),
            pl.BlockSpec((MIX_WIDTH, D_MODEL), const, pipeline_mode=pl.Buffered(1)),
        ],
        out_specs=pl.BlockSpec((tm, D_MODEL), row),
        out_shape=jax.ShapeDtypeStruct((S, D_MODEL), F32),
        compiler_params=pltpu.CompilerParams(
            dimension_semantics=("arbitrary",), vmem_limit_bytes=_vmem_limit(est)),
        name="out_proj",
    )(attn_out, g_attn, h_fwd, h_bwd, g_lru, x2, attn_norm_w, lru_norm_w, w_out_bf16)


def _rope_tables(seq_len, tile_rows):
    rows = seq_len // GRID_W
    inv_freq = ROPE_THETA ** (-jnp.arange(ROPE_PAIRS, dtype=F32) / ROPE_PAIRS)
    ang_r = jnp.arange(rows, dtype=F32)[:, None] * inv_freq[None, :]
    ang_c = jnp.arange(GRID_W, dtype=F32)[:, None] * inv_freq[None, :]
    cr, sr, cc, sc = jnp.cos(ang_r), jnp.sin(ang_r), jnp.cos(ang_c), jnp.sin(ang_c)
    zr, zc = jnp.zeros_like(cr), jnp.zeros_like(cc)
    reps = tile_rows // GRID_W
    return (jnp.concatenate([cr, cr, zr, zr], axis=-1),
            jnp.concatenate([-sr, sr, zr, zr], axis=-1),
            jnp.tile(jnp.concatenate([zc, zc, cc, cc], axis=-1), (reps, 1)),
            jnp.tile(jnp.concatenate([zc, zc, -sc, sc], axis=-1), (reps, 1)))


def _layer(x2, norm_w, w_in, q_norm_w, k_norm_w, conv_w, conv_b, lru_wa, lru_ba, lru_wx, lru_bx,
           lru_lambda, attn_norm_w, lru_norm_w, w_out):
    S = x2.shape[0]
    q_t, k, v_t, k_norm2, g_attn, xr, g_lru = _in_proj(
        x2, norm_w.reshape(1, D_MODEL), w_in.astype(BF16), q_norm_w.reshape(1, HEAD_DIM),
        k_norm_w.reshape(1, HEAD_DIM), *_rope_tables(S, PROJ_TM))
    attn_out = _attention(k_norm2, q_t, k, v_t)
    w_gate = jnp.concatenate([lru_wa, lru_wx], axis=-1).astype(BF16)
    bias_gate = 0.5 * jnp.stack([lru_ba, lru_bx], axis=1)
    h_fwd, h_bwd = _lru(xr, 0.5 * conv_w.reshape(CONV_W, LRU_BLOCKS, LRU_BW),
                        0.5 * conv_b.reshape(LRU_BLOCKS, LRU_BW),
                        w_gate, bias_gate, lru_lambda.reshape(2, 1, LRU_WIDTH))
    return _out_proj(attn_out, g_attn, h_fwd, h_bwd, g_lru, x2,
                     attn_norm_w.reshape(1, ATTN_WIDTH), lru_norm_w.reshape(1, LRU_WIDTH),
                     w_out.astype(BF16))


def kernel(x, norm_w, w_in, q_norm_w, k_norm_w, conv_w, conv_b, lru_wa, lru_ba, lru_wx, lru_bx,
           lru_lambda, attn_norm_w, lru_norm_w, w_out):
    B, S, D = x.shape
    assert D == D_MODEL and S % GRID_W == 0
    outs = [_layer(x[b], norm_w, w_in, q_norm_w, k_norm_w, conv_w, conv_b, lru_wa, lru_ba,
                   lru_wx, lru_bx, lru_lambda, attn_norm_w, lru_norm_w, w_out)
            for b in range(B)]
    return jnp.stack(outs, axis=0) if B > 1 else outs[0][None]
```

```python
import functools
import math

import jax
import jax.numpy as jnp
from jax import lax
from jax.experimental import pallas as pl
from jax.experimental.pallas import tpu as pltpu

F32 = jnp.float32
BF16 = jnp.bfloat16

D_MODEL = 2048
N_HEADS = 8
N_KV_HEADS = 2
GROUP = N_HEADS // N_KV_HEADS
HEAD_DIM = 128
ATTN_WIDTH = N_HEADS * HEAD_DIM
KV_WIDTH = N_KV_HEADS * HEAD_DIM
ROPE_THETA = 10000.0
ROPE_PAIRS = HEAD_DIM // 4
GRID_W = 64
LRU_WIDTH = D_MODEL // 2
LRU_BLOCKS = 8
LRU_BW = LRU_WIDTH // LRU_BLOCKS
LRU_C = 8.0
CONV_W = 4
CONV_LEFT = 2
MIX_WIDTH = ATTN_WIDTH + LRU_WIDTH
IN_WIDTH = 2 * ATTN_WIDTH + 2 * KV_WIDTH + 2 * LRU_WIDTH
EPS = 1e-6

COL_Q = 0
COL_K = ATTN_WIDTH
COL_V = ATTN_WIDTH + KV_WIDTH
COL_GA = ATTN_WIDTH + 2 * KV_WIDTH
COL_XR = COL_GA + ATTN_WIDTH
COL_GL = COL_XR + LRU_WIDTH

SUBLANES = 8
LANES = 128
VMEM_LIMIT_CAP = 60000 * 1024
MIB = 1024 * 1024

NEG_BIG = -1e30
TINY = 1e-30
LOG2_E = math.log2(math.e)


def _vmem_limit(estimate_bytes):
    return int(min(VMEM_LIMIT_CAP, estimate_bytes + 8 * MIB))


def _sigmoid(x):
    return 0.5 * jnp.tanh(0.5 * x) + 0.5


PROJ_TM = 256
PROJ_TN = 512
KN_ROWS = SUBLANES // N_KV_HEADS


def _in_proj_kernel(x_ref, nw_ref, w_ref, qw_ref, kw_ref, rcos_ref, rsin_ref, ccos_ref, csin_ref,
                    q_ref, k_ref, v_ref, kn_ref, ga_ref, xr_ref, gl_ref):
    @pl.when(pl.program_id(0) == 0)
    def _():
        kn_ref[...] = jnp.zeros(kn_ref.shape, F32)

    x = x_ref[...]
    ms = jnp.mean(x * x, axis=-1, keepdims=True)
    inv_rms = lax.rsqrt(ms + EPS)
    h = (x * nw_ref[...]).astype(BF16)

    def rope_table(row_ref, col_ref):
        grid_rows = x_ref.shape[0] // GRID_W
        first = pl.program_id(0) * grid_rows
        lines = [jnp.broadcast_to(row_ref[pl.ds(first + r, 1), :], (GRID_W, HEAD_DIM))
                 for r in range(grid_rows)]
        return jnp.concatenate(lines, axis=0) + col_ref[...]

    cos = rope_table(rcos_ref, ccos_ref)
    sin = rope_table(rsin_ref, csin_ref)
    lane = lax.broadcasted_iota(jnp.int32, (1, HEAD_DIM), 1)
    low_half = (lane % (2 * ROPE_PAIRS)) < ROPE_PAIRS

    def proj(col, width):
        return jnp.dot(h, w_ref[:, col:col + width], preferred_element_type=F32) * inv_rms

    def norm_rope(seg, gain, scale):
        ms_h = jnp.mean(seg * seg, axis=-1, keepdims=True)
        y = seg * lax.rsqrt(ms_h + EPS) * gain
        partner = jnp.where(low_half, pltpu.roll(y, HEAD_DIM - ROPE_PAIRS, 1),
                            pltpu.roll(y, ROPE_PAIRS, 1))
        return (y * cos + partner * sin) * scale

    scale = HEAD_DIM ** -0.5 * LOG2_E
    qw = qw_ref[...]
    kw = kw_ref[...]
    tm = x_ref.shape[0]
    for cb in range(ATTN_WIDTH // PROJ_TN):
        acc = proj(COL_Q + cb * PROJ_TN, PROJ_TN)
        for hh in range(PROJ_TN // HEAD_DIM):
            head = cb * (PROJ_TN // HEAD_DIM) + hh
            seg = acc[:, hh * HEAD_DIM:(hh + 1) * HEAD_DIM]
            qt = norm_rope(seg, qw, scale).T.astype(BF16)
            lane0 = (head % GROUP) * tm
            q_ref[head // GROUP, 0, :, lane0:lane0 + tm] = qt
    acc = proj(COL_K, 2 * KV_WIDTH)
    for hh in range(N_KV_HEADS):
        kb = norm_rope(acc[:, hh * HEAD_DIM:(hh + 1) * HEAD_DIM], kw, 1.0).astype(BF16)
        k_ref[hh] = kb
        kf = kb.astype(F32)
        n2 = jnp.max(jnp.sum(kf * kf, axis=-1, keepdims=True), axis=0, keepdims=True)
        rows = slice(hh * KN_ROWS, (hh + 1) * KN_ROWS)
        kn_ref[rows, :] = jnp.maximum(kn_ref[rows, :], n2)
        vseg = acc[:, KV_WIDTH + hh * HEAD_DIM:KV_WIDTH + (hh + 1) * HEAD_DIM]
        v_ref[hh] = vseg.T.astype(BF16)
    for cb in range(ATTN_WIDTH // PROJ_TN):
        g = proj(COL_GA + cb * PROJ_TN, PROJ_TN)
        ga_ref[:, cb * PROJ_TN:(cb + 1) * PROJ_TN] = (g * _sigmoid(g)).astype(BF16)
    for cb in range(LRU_WIDTH // PROJ_TN):
        g = proj(COL_GL + cb * PROJ_TN, PROJ_TN)
        gl_ref[:, cb * PROJ_TN:(cb + 1) * PROJ_TN] = (g * _sigmoid(g)).astype(BF16)
    for cb in range(LRU_WIDTH // PROJ_TN):
        acc = proj(COL_XR + cb * PROJ_TN, PROJ_TN)
        for q in range(PROJ_TN // LANES):
            n = cb * (PROJ_TN // LANES) + q
            xr_ref[pl.ds(n, tm, stride=LRU_BLOCKS), :] = acc[:, q * LANES:(q + 1) * LANES]


def _in_proj(x2, norm_w, w_in_bf16, q_norm_w, k_norm_w, row_cos, row_sin, col_cos, col_sin):
    S = x2.shape[0]
    tm = PROJ_TM
    assert tm % GRID_W == 0 and S % tm == 0
    const = lambda i: (0, 0)
    est = (2 * tm * D_MODEL * 4 + D_MODEL * IN_WIDTH * 2 + 4 * tm * HEAD_DIM * 4
           + 2 * (3 * tm * ATTN_WIDTH * 2 + 2 * tm * KV_WIDTH * 2 + tm * LRU_WIDTH * 4)
           + tm * D_MODEL * 6 + 4 * tm * PROJ_TN * 4)
    return pl.pallas_call(
        _in_proj_kernel,
        grid=(S // tm,),
        in_specs=[
            pl.BlockSpec((tm, D_MODEL), lambda i: (i, 0)),
            pl.BlockSpec((1, D_MODEL), const),
            pl.BlockSpec((D_MODEL, IN_WIDTH), const, pipeline_mode=pl.Buffered(1)),
            pl.BlockSpec((1, HEAD_DIM), const),
            pl.BlockSpec((1, HEAD_DIM), const),
            pl.BlockSpec((S // GRID_W, HEAD_DIM), const),
            pl.BlockSpec((S // GRID_W, HEAD_DIM), const),
            pl.BlockSpec((tm, HEAD_DIM), const),
            pl.BlockSpec((tm, HEAD_DIM), const),
        ],
        out_specs=[
            pl.BlockSpec((N_KV_HEADS, 1, HEAD_DIM, GROUP * tm), lambda i: (0, i, 0, 0)),
            pl.BlockSpec((N_KV_HEADS, tm, HEAD_DIM), lambda i: (0, i, 0)),
            pl.BlockSpec((N_KV_HEADS, HEAD_DIM, tm), lambda i: (0, 0, i)),
            pl.BlockSpec((N_KV_HEADS * KN_ROWS, LANES), const),
            pl.BlockSpec((tm, ATTN_WIDTH), lambda i: (i, 0)),
            pl.BlockSpec((tm * LRU_BLOCKS, LANES), lambda i: (i, 0)),
            pl.BlockSpec((tm, LRU_WIDTH), lambda i: (i, 0)),
        ],
        out_shape=[
            jax.ShapeDtypeStruct((N_KV_HEADS, S // tm, HEAD_DIM, GROUP * tm), BF16),
            jax.ShapeDtypeStruct((N_KV_HEADS, S, HEAD_DIM), BF16),
            jax.ShapeDtypeStruct((N_KV_HEADS, HEAD_DIM, S), BF16),
            jax.ShapeDtypeStruct((N_KV_HEADS * KN_ROWS, LANES), F32),
            jax.ShapeDtypeStruct((S, ATTN_WIDTH), BF16),
            jax.ShapeDtypeStruct((S * LRU_BLOCKS, LANES), F32),
            jax.ShapeDtypeStruct((S, LRU_WIDTH), BF16),
        ],
        compiler_params=pltpu.CompilerParams(
            dimension_semantics=("arbitrary",), vmem_limit_bytes=_vmem_limit(est)),
        name="in_proj",
    )(x2, norm_w, w_in_bf16, q_norm_w, k_norm_w, row_cos, row_sin, col_cos, col_sin)


ATTN_Q_TILES = 2
ATTN_TK = 2048
ONLINE_TK = 512
TILES_PER_TRIP = 2


SOFTMAX_ROWS = 32


def _sublane_allmax(x):
    for shift in (4, 2, 1):
        x = jnp.maximum(x, pltpu.roll(x, shift, 0))
    return x


SAFE_LOGIT_BOUND = 55.0
BOUND_SLACK = 1.0 + 2.0 ** -10
LANE_BLOCK = 256


def _attn_kernel(kn_ref, q_tiles_ref, k_ref, vt_ref, o_ref, qt_ref, m_ref, l_ref, acc_ref, s_buf,
                 tmax_buf, p_buf, alpha_buf, *, tk):
    n_tiles, _, tile_lanes = q_tiles_ref.shape
    tile_rows = tile_lanes // GROUP
    lanes = n_tiles * tile_lanes
    n_kv = k_ref.shape[0] // tk
    n_groups = tk // SUBLANES
    chunk = SOFTMAX_ROWS // SUBLANES
    acc_shape = (HEAD_DIM // SUBLANES, SUBLANES, lanes)

    for t in range(n_tiles):
        qt_ref[:, t * tile_lanes:(t + 1) * tile_lanes] = q_tiles_ref[t]
    qf = qt_ref[...].astype(F32)
    qn2 = jnp.sum(qf * qf, axis=0, keepdims=True)
    kn = kn_ref[...]
    kn2 = jnp.where(pl.program_id(0) == 0, kn[0:1, :], kn[KN_ROWS:KN_ROWS + 1, :])
    bound = jnp.sqrt(qn2 * jnp.tile(kn2, (1, lanes // LANES))) * BOUND_SLACK
    single_pass = jnp.max(bound) <= SAFE_LOGIT_BOUND

    def kv_rows(j, size=tk):
        return pl.ds(pl.multiple_of(j * size, size), size)

    def finish():
        l_tot = jnp.sum(l_ref[...], axis=0, keepdims=True)
        out_t = acc_ref[...].reshape(HEAD_DIM, lanes) * (1.0 / l_tot)
        for t in range(n_tiles):
            for h in range(GROUP):
                lane0 = t * tile_lanes + h * tile_rows
                o_ref[t * tile_rows:(t + 1) * tile_rows, h * HEAD_DIM:(h + 1) * HEAD_DIM] = (
                    out_t[:, lane0:lane0 + tile_rows].T.astype(o_ref.dtype))

    l_ref[...] = jnp.zeros((SUBLANES, lanes), F32)
    acc_ref[...] = jnp.zeros(acc_shape, F32)

    @pl.when(single_pass)
    def _():
        m_ref[...] = jnp.broadcast_to(bound, (SUBLANES, lanes))

        def probs(j, slot):
            kt = k_ref[kv_rows(j), :]
            for lb in range(lanes // LANE_BLOCK):
                cols = slice(lb * LANE_BLOCK, (lb + 1) * LANE_BLOCK)
                s = jnp.dot(kt, qt_ref[:, cols], preferred_element_type=F32)
                p = jnp.exp2(s.reshape(n_groups, SUBLANES, LANE_BLOCK) - m_ref[:, cols])
                l_ref[:, cols] += jnp.sum(p, axis=0)
                p_buf[slot, :, cols] = p.reshape(tk, LANE_BLOCK).astype(BF16)

        def values(j, slot):
            pv = jnp.dot(vt_ref[:, kv_rows(j)], p_buf[slot], preferred_element_type=F32)
            acc_ref[...] += pv.reshape(acc_shape)

        probs(0, 0)

        def body(t, _):
            base = TILES_PER_TRIP * t
            for u in range(TILES_PER_TRIP):
                probs(base + u + 1, (u + 1) % 2)
                values(base + u, u % 2)
            return 0

        lax.fori_loop(0, (n_kv - 2) // TILES_PER_TRIP, body, 0)
        probs(n_kv - 1, 1)
        values(n_kv - 2, 0)
        values(n_kv - 1, 1)
        finish()

    @pl.when(jnp.logical_not(single_pass))
    def _():
        groups_o = s_buf.shape[1]
        tko = groups_o * SUBLANES
        n_kvo = k_ref.shape[0] // tko

        def scores(j, slot):
            s = jnp.dot(k_ref[kv_rows(j, tko), :], qt_ref[...], preferred_element_type=F32)
            s = s.reshape(groups_o, SUBLANES, lanes)
            s_buf[slot] = s
            tmax_buf[slot] = jnp.max(s, axis=0)

        def softmax(slot):
            m_old = m_ref[...]
            m_new = jnp.maximum(m_old, _sublane_allmax(tmax_buf[slot]))
            alpha = jnp.exp2(m_old - m_new)
            psum = jnp.zeros((SUBLANES, lanes), F32)
            for c in range(groups_o // chunk):
                s = s_buf[slot, c * chunk:(c + 1) * chunk]
                p = jnp.exp2(s - m_new)
                psum = psum + jnp.sum(p, axis=0)
                p_buf[slot, c * SOFTMAX_ROWS:(c + 1) * SOFTMAX_ROWS, :] = (
                    p.reshape(SOFTMAX_ROWS, lanes).astype(BF16))
            l_ref[...] = alpha * l_ref[...] + psum
            m_ref[...] = m_new
            alpha_buf[slot] = alpha

        def values(j, slot):
            pv = jnp.dot(vt_ref[:, kv_rows(j, tko)], p_buf[slot, 0:tko, :],
                         preferred_element_type=F32)
            acc_ref[...] = alpha_buf[slot] * acc_ref[...] + pv.reshape(acc_shape)

        m_ref[...] = jnp.full((SUBLANES, lanes), NEG_BIG, F32)
        p_buf[1, 0:tko, :] = jnp.zeros((tko, lanes), BF16)
        alpha_buf[1] = jnp.ones((SUBLANES, lanes), F32)
        scores(0, 0)

        def body(t, _):
            for slot in range(2):
                cur = 2 * t + slot
                nxt = jnp.where(cur + 1 < n_kvo, cur + 1, 0)
                prv = jnp.maximum(cur - 1, 0)
                scores(nxt, 1 - slot)
                softmax(slot)
                values(prv, 1 - slot)
            return 0

        lax.fori_loop(0, n_kvo // 2, body, 0)
        values(n_kvo - 1, 1)
        finish()


def _attention(k_norm2, q_t, k, v_t):
    S = k.shape[1]
    tk = ATTN_TK
    n_tiles = ATTN_Q_TILES
    tile_lanes = q_t.shape[-1]
    tq = n_tiles * (tile_lanes // GROUP)
    tko = ONLINE_TK
    assert S % (2 * tk) == 0 and S % (2 * tko) == 0 and tko % SOFTMAX_ROWS == 0 and S % tq == 0
    assert TILES_PER_TRIP % 2 == 0 and (S // tk - 2) % TILES_PER_TRIP == 0 and tko <= tk
    lanes = n_tiles * tile_lanes
    est = (3 * HEAD_DIM * lanes * 2 + 2 * 2 * S * HEAD_DIM * 2 + 2 * tq * GROUP * HEAD_DIM * 2
           + (HEAD_DIM + 6 * SUBLANES) * lanes * 4 + 2 * tk * lanes * 2 + 2 * tko * lanes * 4
           + 2 * tk * LANE_BLOCK * 4 + 2 * tko * lanes * 4)
    return pl.pallas_call(
        functools.partial(_attn_kernel, tk=tk),
        grid=(N_KV_HEADS, S // tq),
        in_specs=[
            pl.BlockSpec((N_KV_HEADS * KN_ROWS, LANES), lambda g, i: (0, 0)),
            pl.BlockSpec((None, n_tiles, HEAD_DIM, tile_lanes), lambda g, i: (g, i, 0, 0)),
            pl.BlockSpec((None, S, HEAD_DIM), lambda g, i: (g, 0, 0)),
            pl.BlockSpec((None, HEAD_DIM, S), lambda g, i: (g, 0, 0)),
        ],
        out_specs=pl.BlockSpec((tq, GROUP * HEAD_DIM), lambda g, i: (i, g)),
        out_shape=jax.ShapeDtypeStruct((S, ATTN_WIDTH), BF16),
        scratch_shapes=[
            pltpu.VMEM((HEAD_DIM, lanes), BF16),
            pltpu.VMEM((SUBLANES, lanes), F32),
            pltpu.VMEM((SUBLANES, lanes), F32),
            pltpu.VMEM((HEAD_DIM // SUBLANES, SUBLANES, lanes), F32),
            pltpu.VMEM((2, tko // SUBLANES, SUBLANES, lanes), F32),
            pltpu.VMEM((2, SUBLANES, lanes), F32),
            pltpu.VMEM((2, tk, lanes), BF16),
            pltpu.VMEM((2, SUBLANES, lanes), F32),
        ],
        compiler_params=pltpu.CompilerParams(
            dimension_semantics=("arbitrary", "arbitrary"), vmem_limit_bytes=_vmem_limit(est)),
        name="attention",
    )(k_norm2, q_t, k, v_t)


LRU_T = 512
SCAN_UNROLL = 16
assert LRU_WIDTH == SUBLANES * LANES and LRU_BLOCKS == SUBLANES


def _lru_gates(prev_ref, cur_ref, next_ref, ext_tm, xc_tm, a_tm, b_tm, has_prev, has_next,
               cw_ref, cb_ref, w_ref, bias_ref, lam_ref):
    rows = cur_ref.shape[0]
    T = rows // LRU_BLOCKS
    halo = prev_ref.shape[0]
    ext_tm[0:halo, :] = jnp.where(has_prev, prev_ref[...], 0.0)
    ext_tm[halo:halo + rows, :] = cur_ref[...]
    ext_tm[halo + rows:2 * halo + rows, :] = jnp.where(has_next, next_ref[...], 0.0)
    base = halo - CONV_LEFT * LRU_BLOCKS
    xh = cb_ref[...]
    for j in range(CONV_W):
        start = base + j * LRU_BLOCKS
        tap = ext_tm[start:start + rows, :].reshape(T, LRU_BLOCKS, LRU_BW)
        xh = xh + cw_ref[j] * tap
    xc_tm[...] = xh.reshape(rows, LRU_BW)
    lam = lam_ref[...]
    neg = -lam
    softplus = jnp.maximum(neg, 0.0) + jnp.log1p(jnp.exp(-jnp.abs(neg)))
    half_rate = (-0.5 * LRU_C * LOG2_E) * softplus
    for n in range(LRU_BLOCKS):
        cols = slice(n * LRU_BW, (n + 1) * LRU_BW)
        block_n = pl.ds(n, T, stride=LRU_BLOCKS)
        xb = xc_tm[block_n, :]
        z = jnp.dot(xb.astype(BF16), w_ref[n], preferred_element_type=F32)
        tanh_r = jnp.tanh(z[:, :LRU_BW] + bias_ref[0:1, cols])
        tanh_i = jnp.tanh(z[:, LRU_BW:] + bias_ref[1:2, cols])
        rate = half_rate[:, cols]
        a = jnp.exp2(tanh_r * rate + rate)
        one_minus_a2 = 1.0 - a * a
        mult = one_minus_a2 * lax.rsqrt(jnp.maximum(one_minus_a2, TINY))
        half_in = mult * xb
        a_tm[block_n, :] = a
        b_tm[block_n, :] = half_in * tanh_i + half_in


def _lru_scan(af_tm, bf_tm, ab_tm, bb_tm, hf_ref, hb_ref, tiles_f, tiles_b, carry_ref):
    T = af_tm.shape[0] // LRU_BLOCKS
    U = SCAN_UNROLL
    tile_rows = LRU_BLOCKS * SUBLANES

    def step_rows(base_t, off):
        return pl.ds(pl.multiple_of(base_t * LRU_BLOCKS, U * LRU_BLOCKS) + off * LRU_BLOCKS,
                     LRU_BLOCKS)

    def scatter(tiles, base_t, off, h):
        start = (pl.multiple_of(base_t * LRU_BLOCKS, U * LRU_BLOCKS)
                 + (off // SUBLANES) * tile_rows + off % SUBLANES)
        tiles[pl.ds(start, LRU_BLOCKS, stride=SUBLANES), :] = h

    def pair(a_tm, b_tm, tiles, base_t, off0, off1, h):
        a0, b0 = a_tm[step_rows(base_t, off0), :], b_tm[step_rows(base_t, off0), :]
        a1, b1 = a_tm[step_rows(base_t, off1), :], b_tm[step_rows(base_t, off1), :]
        h0 = a0 * h + b0
        h1 = (a1 * a0) * h + (a1 * b0 + b1)
        scatter(tiles, base_t, off0, h0)
        scatter(tiles, base_t, off1, h1)
        return h1

    def store(out_ref, tiles, base_t):
        grp = tiles[pl.ds(pl.multiple_of(base_t * LRU_BLOCKS, U * LRU_BLOCKS), U * LRU_BLOCKS), :]
        grp = grp.reshape(U // SUBLANES, LRU_BLOCKS, SUBLANES, LRU_BW)
        rows = pl.ds(pl.multiple_of(base_t, U), U)
        for n in range(LRU_BLOCKS):
            out_ref[rows, n * LRU_BW:(n + 1) * LRU_BW] = (
                grp[:, n].reshape(U, LRU_BW).astype(out_ref.dtype))

    def body(i, carry):
        h_f, h_b = carry
        base_f = i * U
        base_b = T - U - i * U
        for u in range(0, U, 2):
            h_f = pair(af_tm, bf_tm, tiles_f, base_f, u, u + 1, h_f)
            h_b = pair(ab_tm, bb_tm, tiles_b, base_b, U - 1 - u, U - 2 - u, h_b)
        store(hf_ref, tiles_f, base_f)
        store(hb_ref, tiles_b, base_b)
        return h_f, h_b

    h_f, h_b = lax.fori_loop(0, T // U, body, (carry_ref[0], carry_ref[1]))
    carry_ref[0] = h_f
    carry_ref[1] = h_b


def _lru_kernel(fp_ref, fc_ref, fn_ref, bp_ref, bc_ref, bn_ref, cw_ref, cb_ref, w_ref, bias_ref,
                lam_ref, hf_ref, hb_ref, ext_tm, xc_tm, af_tm, bf_tm, ab_tm, bb_tm, tiles_f,
                tiles_b, carry_ref):
    c = pl.program_id(0)
    n_chunks = pl.num_programs(0)

    @pl.when(c == 0)
    def _():
        carry_ref[...] = jnp.zeros(carry_ref.shape, F32)

    _lru_gates(fp_ref, fc_ref, fn_ref, ext_tm, xc_tm, af_tm, bf_tm, c > 0, c < n_chunks - 1,
               cw_ref, cb_ref, w_ref.at[0], bias_ref.at[0], lam_ref.at[0])
    _lru_gates(bp_ref, bc_ref, bn_ref, ext_tm, xc_tm, ab_tm, bb_tm, c < n_chunks - 1, c > 0,
               cw_ref, cb_ref, w_ref.at[1], bias_ref.at[1], lam_ref.at[1])
    _lru_scan(af_tm, bf_tm, ab_tm, bb_tm, hf_ref, hb_ref, tiles_f, tiles_b, carry_ref)


LRU_HALO = SUBLANES


def _lru(xr_tm, conv_w_tm, conv_b_tm, w_gate, bias_gate, lam):
    S = xr_tm.shape[0] // LRU_BLOCKS
    T = LRU_T
    assert S % T == 0 and T % SCAN_UNROLL == 0 and T % LRU_HALO == 0
    assert CONV_LEFT <= LRU_HALO and CONV_W - 1 - CONV_LEFT <= LRU_HALO
    time_major = pltpu.VMEM((T * LRU_BLOCKS, LRU_BW), F32)
    n_chunks = S // T
    per = T // LRU_HALO
    last_blk = S // LRU_HALO - 1
    W = LRU_WIDTH
    chunk = (T * LRU_BLOCKS, LRU_BW)
    halo = (LRU_HALO * LRU_BLOCKS, LRU_BW)
    const2 = lambda c: (0, 0)
    est = (2 * 2 * (T * W * 4 + 2 * SUBLANES * W * 4) + 2 * 2 * T * W * 4
           + 2 * 2 * LRU_BLOCKS * LRU_BW * 2 * LRU_BW * 2 + (8 * T + 64) * W * 4 + 6 * T * W * 4)
    return pl.pallas_call(
        _lru_kernel,
        grid=(n_chunks,),
        in_specs=[
            pl.BlockSpec(halo, lambda c: (jnp.maximum(c * per - 1, 0), 0)),
            pl.BlockSpec(chunk, lambda c: (c, 0)),
            pl.BlockSpec(halo, lambda c: (jnp.minimum((c + 1) * per, last_blk), 0)),
            pl.BlockSpec(halo, lambda c: (jnp.maximum((n_chunks - 1 - c) * per - 1, 0), 0)),
            pl.BlockSpec(chunk, lambda c: (n_chunks - 1 - c, 0)),
            pl.BlockSpec(halo, lambda c: (jnp.minimum((n_chunks - c) * per, last_blk), 0)),
            pl.BlockSpec((CONV_W, LRU_BLOCKS, LRU_BW), lambda c: (0, 0, 0)),
            pl.BlockSpec((LRU_BLOCKS, LRU_BW), const2),
            pl.BlockSpec((2, LRU_BLOCKS, LRU_BW, 2 * LRU_BW), lambda c: (0, 0, 0, 0)),
            pl.BlockSpec((2, 2, W), lambda c: (0, 0, 0)),
            pl.BlockSpec((2, 1, W), lambda c: (0, 0, 0)),
        ],
        out_specs=[
            pl.BlockSpec((T, W), lambda c: (c, 0)),
            pl.BlockSpec((T, W), lambda c: (n_chunks - 1 - c, 0)),
        ],
        out_shape=[jax.ShapeDtypeStruct((S, W), BF16), jax.ShapeDtypeStruct((S, W), BF16)],
        scratch_shapes=[
            pltpu.VMEM(((T + 2 * LRU_HALO) * LRU_BLOCKS, LRU_BW), F32),
            time_major,
            time_major, time_major,
            time_major, time_major,
            time_major, time_major,
            pltpu.VMEM((2, SUBLANES, LRU_BW), F32),
        ],
        compiler_params=pltpu.CompilerParams(
            dimension_semantics=("arbitrary",), vmem_limit_bytes=_vmem_limit(est)),
        name="rg_lru",
    )(xr_tm, xr_tm, xr_tm, xr_tm, xr_tm, xr_tm, conv_w_tm, conv_b_tm, w_gate, bias_gate, lam)


OUT_TM = 512


def _out_proj_kernel(ao_ref, ga_ref, hf_ref, hb_ref, gl_ref, x_ref, anw_ref, lnw_ref, w_ref,
                     o_ref):
    def inv_rms(v):
        return lax.rsqrt(jnp.mean(v * v, axis=-1, keepdims=True) + EPS)

    ao = ao_ref[...].astype(F32)
    hl = hf_ref[...].astype(F32) + hb_ref[...].astype(F32)
    ya = (ao * anw_ref[...] * ga_ref[...].astype(F32)).astype(BF16)
    yl = (hl * lnw_ref[...] * gl_ref[...].astype(F32)).astype(BF16)
    mix_a = jnp.dot(ya, w_ref[0:ATTN_WIDTH, :], preferred_element_type=F32) * inv_rms(ao)
    mix_l = jnp.dot(yl, w_ref[ATTN_WIDTH:MIX_WIDTH, :], preferred_element_type=F32) * inv_rms(hl)
    o_ref[...] = x_ref[...] + mix_a + mix_l


def _out_proj(attn_out, g_attn, h_fwd, h_bwd, g_lru, x2, attn_norm_w, lru_norm_w, w_out_bf16):
    S = x2.shape[0]
    tm = OUT_TM
    row = lambda i: (i, 0)
    const = lambda i: (0, 0)
    est = (2 * 5 * tm * ATTN_WIDTH * 2 + 2 * 2 * tm * D_MODEL * 4 + MIX_WIDTH * D_MODEL * 2
           + 4 * tm * D_MODEL * 4)
    return pl.pallas_call(
        _out_proj_kernel,
        grid=(S // tm,),
        in_specs=[
            pl.BlockSpec((tm, ATTN_WIDTH), row),
            pl.BlockSpec((tm, ATTN_WIDTH), row),
            pl.BlockSpec((tm, LRU_WIDTH), row),
            pl.BlockSpec((tm, LRU_WIDTH), row),
            pl.BlockSpec((tm, LRU_WIDTH), row),
            pl.BlockSpec((tm, D_MODEL), row),
            pl.BlockSpec((1, ATTN_WIDTH), const),
            pl.BlockSpec((1, LRU_WIDTH), const),
            pl.BlockSpec((MIX_WIDTH, D_MODEL), const, pipeline_mode=pl.Buffered(1)),
        ],
        out_specs=pl.BlockSpec((tm, D_MODEL), row),
        out_shape=jax.ShapeDtypeStruct((S, D_MODEL), F32),
        compiler_params=pltpu.CompilerParams(
            dimension_semantics=("arbitrary",), vmem_limit_bytes=_vmem_limit(est)),
        name="out_proj",
    )(attn_out, g_attn, h_fwd, h_bwd, g_lru, x2, attn_norm_w, lru_norm_w, w_out_bf16)


def _rope_tables(seq_len, tile_rows):
    rows = seq_len // GRID_W
    inv_freq = ROPE_THETA ** (-jnp.arange(ROPE_PAIRS, dtype=F32) / ROPE_PAIRS)
    ang_r = jnp.arange(rows, dtype=F32)[:, None] * inv_freq[None, :]
    ang_c = jnp.arange(GRID_W, dtype=F32)[:, None] * inv_freq[None, :]
    cr, sr, cc, sc = jnp.cos(ang_r), jnp.sin(ang_r), jnp.cos(ang_c), jnp.sin(ang_c)
    zr, zc = jnp.zeros_like(cr), jnp.zeros_like(cc)
    reps = tile_rows // GRID_W
    return (jnp.concatenate([cr, cr, zr, zr], axis=-1),
            jnp.concatenate([-sr, sr, zr, zr], axis=-1),
            jnp.tile(jnp.concatenate([zc, zc, cc, cc], axis=-1), (reps, 1)),
            jnp.tile(jnp.concatenate([zc, zc, -sc, sc], axis=-1), (reps, 1)))


def _layer(x2, norm_w, w_in, q_norm_w, k_norm_w, conv_w, conv_b, lru_wa, lru_ba, lru_wx, lru_bx,
           lru_lambda, attn_norm_w, lru_norm_w, w_out):
    S = x2.shape[0]
    q_t, k, v_t, k_norm2, g_attn, xr, g_lru = _in_proj(
        x2, norm_w.reshape(1, D_MODEL), w_in.astype(BF16), q_norm_w.reshape(1, HEAD_DIM),
        k_norm_w.reshape(1, HEAD_DIM), *_rope_tables(S, PROJ_TM))
    attn_out = _attention(k_norm2, q_t, k, v_t)
    w_gate = jnp.concatenate([lru_wa, lru_wx], axis=-1).astype(BF16)
    bias_gate = 0.5 * jnp.stack([lru_ba, lru_bx], axis=1)
    h_fwd, h_bwd = _lru(xr, 0.5 * conv_w.reshape(CONV_W, LRU_BLOCKS, LRU_BW),
                        0.5 * conv_b.reshape(LRU_BLOCKS, LRU_BW),
                        w_gate, bias_gate, lru_lambda.reshape(2, 1, LRU_WIDTH))
    return _out_proj(attn_out, g_attn, h_fwd, h_bwd, g_lru, x2,
                     attn_norm_w.reshape(1, ATTN_WIDTH), lru_norm_w.reshape(1, LRU_WIDTH),
                     w_out.astype(BF16))


def kernel(x, norm_w, w_in, q_norm_w, k_norm_w, conv_w, conv_b, lru_wa, lru_ba, lru_wx, lru_bx,
           lru_lambda, attn_norm_w, lru_norm_w, w_out):
    B, S, D = x.shape
    assert D == D_MODEL and S % GRID_W == 0
    outs = [_layer(x[b], norm_w, w_in, q_norm_w, k_norm_w, conv_w, conv_b, lru_wa, lru_ba,
                   lru_wx, lru_bx, lru_lambda, attn_norm_w, lru_norm_w, w_out)
            for b in range(B)]
    return jnp.stack(outs, axis=0) if B > 1 else outs[0][None]
```

```python
import functools
import math

import jax
import jax.numpy as jnp
from jax import lax
from jax.experimental import pallas as pl
from jax.experimental.pallas import tpu as pltpu

F32 = jnp.float32
BF16 = jnp.bfloat16

D_MODEL = 2048
N_HEADS = 8
N_KV_HEADS = 2
GROUP = N_HEADS // N_KV_HEADS
HEAD_DIM = 128
ATTN_WIDTH = N_HEADS * HEAD_DIM
KV_WIDTH = N_KV_HEADS * HEAD_DIM
ROPE_THETA = 10000.0
ROPE_PAIRS = HEAD_DIM // 4
GRID_W = 64
LRU_WIDTH = D_MODEL // 2
LRU_BLOCKS = 8
LRU_BW = LRU_WIDTH // LRU_BLOCKS
LRU_C = 8.0
CONV_W = 4
CONV_LEFT = 2
MIX_WIDTH = ATTN_WIDTH + LRU_WIDTH
IN_WIDTH = 2 * ATTN_WIDTH + 2 * KV_WIDTH + 2 * LRU_WIDTH
EPS = 1e-6

COL_Q = 0
COL_K = ATTN_WIDTH
COL_V = ATTN_WIDTH + KV_WIDTH
COL_GA = ATTN_WIDTH + 2 * KV_WIDTH
COL_XR = COL_GA + ATTN_WIDTH
COL_GL = COL_XR + LRU_WIDTH

SUBLANES = 8
LANES = 128
VMEM_LIMIT_CAP = 60000 * 1024
MIB = 1024 * 1024

NEG_BIG = -1e30
TINY = 1e-30
LOG2_E = math.log2(math.e)


def _vmem_limit(estimate_bytes):
    return int(min(VMEM_LIMIT_CAP, estimate_bytes + 8 * MIB))


def _sigmoid(x):
    return 0.5 * jnp.tanh(0.5 * x) + 0.5


PROJ_TM = 256
PROJ_TN = 512
KN_ROWS = SUBLANES // N_KV_HEADS


def _in_proj_kernel(x_ref, nw_ref, w_ref, qw_ref, kw_ref, rcos_ref, rsin_ref, ccos_ref, csin_ref,
                    q_ref, k_ref, v_ref, kn_ref, ga_ref, xr_ref, gl_ref):
    @pl.when(pl.program_id(0) == 0)
    def _():
        kn_ref[...] = jnp.zeros(kn_ref.shape, F32)

    x = x_ref[...]
    ms = jnp.mean(x * x, axis=-1, keepdims=True)
    inv_rms = lax.rsqrt(ms + EPS)
    h = (x * nw_ref[...]).astype(BF16)

    def rope_table(row_ref, col_ref):
        grid_rows = x_ref.shape[0] // GRID_W
        first = pl.program_id(0) * grid_rows
        lines = [jnp.broadcast_to(row_ref[pl.ds(first + r, 1), :], (GRID_W, HEAD_DIM))
                 for r in range(grid_rows)]
        return jnp.concatenate(lines, axis=0) + col_ref[...]

    cos = rope_table(rcos_ref, ccos_ref)
    sin = rope_table(rsin_ref, csin_ref)
    lane = lax.broadcasted_iota(jnp.int32, (1, HEAD_DIM), 1)
    low_half = (lane % (2 * ROPE_PAIRS)) < ROPE_PAIRS

    def proj(col, width):
        return jnp.dot(h, w_ref[:, col:col + width], preferred_element_type=F32) * inv_rms

    def norm_rope(seg, gain, scale):
        ms_h = jnp.mean(seg * seg, axis=-1, keepdims=True)
        y = seg * lax.rsqrt(ms_h + EPS) * gain
        partner = jnp.where(low_half, pltpu.roll(y, HEAD_DIM - ROPE_PAIRS, 1),
                            pltpu.roll(y, ROPE_PAIRS, 1))
        return (y * cos + partner * sin) * scale

    scale = HEAD_DIM ** -0.5 * LOG2_E
    qw = qw_ref[...]
    kw = kw_ref[...]
    tm = x_ref.shape[0]
    for cb in range(ATTN_WIDTH // PROJ_TN):
        acc = proj(COL_Q + cb * PROJ_TN, PROJ_TN)
        for hh in range(PROJ_TN // HEAD_DIM):
            head = cb * (PROJ_TN // HEAD_DIM) + hh
            seg = acc[:, hh * HEAD_DIM:(hh + 1) * HEAD_DIM]
            qt = norm_rope(seg, qw, scale).T.astype(BF16)
            lane0 = (head % GROUP) * tm
            q_ref[head // GROUP, 0, :, lane0:lane0 + tm] = qt
    acc = proj(COL_K, 2 * KV_WIDTH)
    for hh in range(N_KV_HEADS):
        kb = norm_rope(acc[:, hh * HEAD_DIM:(hh + 1) * HEAD_DIM], kw, 1.0).astype(BF16)
        k_ref[hh] = kb
        kf = kb.astype(F32)
        n2 = jnp.max(jnp.sum(kf * kf, axis=-1, keepdims=True), axis=0, keepdims=True)
        rows = slice(hh * KN_ROWS, (hh + 1) * KN_ROWS)
        kn_ref[rows, :] = jnp.maximum(kn_ref[rows, :], n2)
        vseg = acc[:, KV_WIDTH + hh * HEAD_DIM:KV_WIDTH + (hh + 1) * HEAD_DIM]
        v_ref[hh] = vseg.T.astype(BF16)
    for cb in range(ATTN_WIDTH // PROJ_TN):
        g = proj(COL_GA + cb * PROJ_TN, PROJ_TN)
        ga_ref[:, cb * PROJ_TN:(cb + 1) * PROJ_TN] = (g * _sigmoid(g)).astype(BF16)
    for cb in range(LRU_WIDTH // PROJ_TN):
        g = proj(COL_GL + cb * PROJ_TN, PROJ_TN)
        gl_ref[:, cb * PROJ_TN:(cb + 1) * PROJ_TN] = (g * _sigmoid(g)).astype(BF16)
    for cb in range(LRU_WIDTH // PROJ_TN):
        acc = proj(COL_XR + cb * PROJ_TN, PROJ_TN)
        for q in range(PROJ_TN // LANES):
            n = cb * (PROJ_TN // LANES) + q
            xr_ref[pl.ds(n, tm, stride=LRU_BLOCKS), :] = acc[:, q * LANES:(q + 1) * LANES]


def _in_proj(x2, norm_w, w_in_bf16, q_norm_w, k_norm_w, row_cos, row_sin, col_cos, col_sin):
    S = x2.shape[0]
    tm = PROJ_TM
    assert tm % GRID_W == 0 and S % tm == 0
    const = lambda i: (0, 0)
    est = (2 * tm * D_MODEL * 4 + D_MODEL * IN_WIDTH * 2 + 4 * tm * HEAD_DIM * 4
           + 2 * (3 * tm * ATTN_WIDTH * 2 + 2 * tm * KV_WIDTH * 2 + tm * LRU_WIDTH * 4)
           + tm * D_MODEL * 6 + 4 * tm * PROJ_TN * 4)
    return pl.pallas_call(
        _in_proj_kernel,
        grid=(S // tm,),
        in_specs=[
            pl.BlockSpec((tm, D_MODEL), lambda i: (i, 0)),
            pl.BlockSpec((1, D_MODEL), const),
            pl.BlockSpec((D_MODEL, IN_WIDTH), const, pipeline_mode=pl.Buffered(1)),
            pl.BlockSpec((1, HEAD_DIM), const),
            pl.BlockSpec((1, HEAD_DIM), const),
            pl.BlockSpec((S // GRID_W, HEAD_DIM), const),
            pl.BlockSpec((S // GRID_W, HEAD_DIM), const),
            pl.BlockSpec((tm, HEAD_DIM), const),
            pl.BlockSpec((tm, HEAD_DIM), const),
        ],
        out_specs=[
            pl.BlockSpec((N_KV_HEADS, 1, HEAD_DIM, GROUP * tm), lambda i: (0, i, 0, 0)),
            pl.BlockSpec((N_KV_HEADS, tm, HEAD_DIM), lambda i: (0, i, 0)),
            pl.BlockSpec((N_KV_HEADS, HEAD_DIM, tm), lambda i: (0, 0, i)),
            pl.BlockSpec((N_KV_HEADS * KN_ROWS, LANES), const),
            pl.BlockSpec((tm, ATTN_WIDTH), lambda i: (i, 0)),
            pl.BlockSpec((tm * LRU_BLOCKS, LANES), lambda i: (i, 0)),
            pl.BlockSpec((tm, LRU_WIDTH), lambda i: (i, 0)),
        ],
        out_shape=[
            jax.ShapeDtypeStruct((N_KV_HEADS, S // tm, HEAD_DIM, GROUP * tm), BF16),
            jax.ShapeDtypeStruct((N_KV_HEADS, S, HEAD_DIM), BF16),
            jax.ShapeDtypeStruct((N_KV_HEADS, HEAD_DIM, S), BF16),
            jax.ShapeDtypeStruct((N_KV_HEADS * KN_ROWS, LANES), F32),
            jax.ShapeDtypeStruct((S, ATTN_WIDTH), BF16),
            jax.ShapeDtypeStruct((S * LRU_BLOCKS, LANES), F32),
            jax.ShapeDtypeStruct((S, LRU_WIDTH), BF16),
        ],
        compiler_params=pltpu.CompilerParams(
            dimension_semantics=("arbitrary",), vmem_limit_bytes=_vmem_limit(est)),
        name="in_proj",
    )(x2, norm_w, w_in_bf16, q_norm_w, k_norm_w, row_cos, row_sin, col_cos, col_sin)


ATTN_Q_TILES = 2
ATTN_TK = 2048
ONLINE_TK = 512
TILES_PER_TRIP = 6


SOFTMAX_ROWS = 32


def _sublane_allmax(x):
    for shift in (4, 2, 1):
        x = jnp.maximum(x, pltpu.roll(x, shift, 0))
    return x


SAFE_LOGIT_BOUND = 55.0
BOUND_SLACK = 1.0 + 2.0 ** -10
LANE_BLOCK = 256


def _attn_kernel(kn_ref, q_tiles_ref, k_ref, vt_ref, o_ref, qt_ref, m_ref, l_ref, acc_ref, s_buf,
                 tmax_buf, p_buf, alpha_buf, *, tk):
    n_tiles, _, tile_lanes = q_tiles_ref.shape
    tile_rows = tile_lanes // GROUP
    lanes = n_tiles * tile_lanes
    n_kv = k_ref.shape[0] // tk
    n_groups = tk // SUBLANES
    chunk = SOFTMAX_ROWS // SUBLANES
    acc_shape = (HEAD_DIM // SUBLANES, SUBLANES, lanes)

    for t in range(n_tiles):
        qt_ref[:, t * tile_lanes:(t + 1) * tile_lanes] = q_tiles_ref[t]
    qf = qt_ref[...].astype(F32)
    qn2 = jnp.sum(qf * qf, axis=0, keepdims=True)
    kn = kn_ref[...]
    kn2 = jnp.where(pl.program_id(0) == 0, kn[0:1, :], kn[KN_ROWS:KN_ROWS + 1, :])
    bound = jnp.sqrt(qn2 * jnp.tile(kn2, (1, lanes // LANES))) * BOUND_SLACK
    single_pass = jnp.max(bound) <= SAFE_LOGIT_BOUND

    def kv_rows(j, size=tk):
        return pl.ds(pl.multiple_of(j * size, size), size)

    def finish():
        l_tot = jnp.sum(l_ref[...], axis=0, keepdims=True)
        out_t = acc_ref[...].reshape(HEAD_DIM, lanes) * (1.0 / l_tot)
        for t in range(n_tiles):
            for h in range(GROUP):
                lane0 = t * tile_lanes + h * tile_rows
                o_ref[t * tile_rows:(t + 1) * tile_rows, h * HEAD_DIM:(h + 1) * HEAD_DIM] = (
                    out_t[:, lane0:lane0 + tile_rows].T.astype(o_ref.dtype))

    l_ref[...] = jnp.zeros((SUBLANES, lanes), F32)
    acc_ref[...] = jnp.zeros(acc_shape, F32)

    @pl.when(single_pass)
    def _():
        m_ref[...] = jnp.broadcast_to(bound, (SUBLANES, lanes))

        def probs(j, slot):
            kt = k_ref[kv_rows(j), :]
            for lb in range(lanes // LANE_BLOCK):
                cols = slice(lb * LANE_BLOCK, (lb + 1) * LANE_BLOCK)
                s = jnp.dot(kt, qt_ref[:, cols], preferred_element_type=F32)
                p = jnp.exp2(s.reshape(n_groups, SUBLANES, LANE_BLOCK) - m_ref[:, cols])
                l_ref[:, cols] += jnp.sum(p, axis=0)
                p_buf[slot, :, cols] = p.reshape(tk, LANE_BLOCK).astype(BF16)

        def values(j, slot):
            pv = jnp.dot(vt_ref[:, kv_rows(j)], p_buf[slot], preferred_element_type=F32)
            acc_ref[...] += pv.reshape(acc_shape)

        probs(0, 0)

        def body(t, _):
            base = TILES_PER_TRIP * t
            for u in range(TILES_PER_TRIP):
                probs(base + u + 1, (u + 1) % 2)
                values(base + u, u % 2)
            return 0

        lax.fori_loop(0, (n_kv - 2) // TILES_PER_TRIP, body, 0)
        probs(n_kv - 1, 1)
        values(n_kv - 2, 0)
        values(n_kv - 1, 1)
        finish()

    @pl.when(jnp.logical_not(single_pass))
    def _():
        groups_o = s_buf.shape[1]
        tko = groups_o * SUBLANES
        n_kvo = k_ref.shape[0] // tko

        def scores(j, slot):
            s = jnp.dot(k_ref[kv_rows(j, tko), :], qt_ref[...], preferred_element_type=F32)
            s = s.reshape(groups_o, SUBLANES, lanes)
            s_buf[slot] = s
            tmax_buf[slot] = jnp.max(s, axis=0)

        def softmax(slot):
            m_old = m_ref[...]
            m_new = jnp.maximum(m_old, _sublane_allmax(tmax_buf[slot]))
            alpha = jnp.exp2(m_old - m_new)
            psum = jnp.zeros((SUBLANES, lanes), F32)
            for c in range(groups_o // chunk):
                s = s_buf[slot, c * chunk:(c + 1) * chunk]
                p = jnp.exp2(s - m_new)
                psum = psum + jnp.sum(p, axis=0)
                p_buf[slot, c * SOFTMAX_ROWS:(c + 1) * SOFTMAX_ROWS, :] = (
                    p.reshape(SOFTMAX_ROWS, lanes).astype(BF16))
            l_ref[...] = alpha * l_ref[...] + psum
            m_ref[...] = m_new
            alpha_buf[slot] = alpha

        def values(j, slot):
            pv = jnp.dot(vt_ref[:, kv_rows(j, tko)], p_buf[slot, 0:tko, :],
                         preferred_element_type=F32)
            acc_ref[...] = alpha_buf[slot] * acc_ref[...] + pv.reshape(acc_shape)

        m_ref[...] = jnp.full((SUBLANES, lanes), NEG_BIG, F32)
        p_buf[1, 0:tko, :] = jnp.zeros((tko, lanes), BF16)
        alpha_buf[1] = jnp.ones((SUBLANES, lanes), F32)
        scores(0, 0)

        def body(t, _):
            for slot in range(2):
                cur = 2 * t + slot
                nxt = jnp.where(cur + 1 < n_kvo, cur + 1, 0)
                prv = jnp.maximum(cur - 1, 0)
                scores(nxt, 1 - slot)
                softmax(slot)
                values(prv, 1 - slot)
            return 0

        lax.fori_loop(0, n_kvo // 2, body, 0)
        values(n_kvo - 1, 1)
        finish()


def _attention(k_norm2, q_t, k, v_t):
    S = k.shape[1]
    tk = ATTN_TK
    n_tiles = ATTN_Q_TILES
    tile_lanes = q_t.shape[-1]
    tq = n_tiles * (tile_lanes // GROUP)
    tko = ONLINE_TK
    assert S % (2 * tk) == 0 and S % (2 * tko) == 0 and tko % SOFTMAX_ROWS == 0 and S % tq == 0
    assert TILES_PER_TRIP % 2 == 0 and (S // tk - 2) % TILES_PER_TRIP == 0 and tko <= tk
    lanes = n_tiles * tile_lanes
    est = (3 * HEAD_DIM * lanes * 2 + 2 * 2 * S * HEAD_DIM * 2 + 2 * tq * GROUP * HEAD_DIM * 2
           + (HEAD_DIM + 6 * SUBLANES) * lanes * 4 + 2 * tk * lanes * 2 + 2 * tko * lanes * 4
           + 2 * tk * LANE_BLOCK * 4 + 2 * tko * lanes * 4)
    return pl.pallas_call(
        functools.partial(_attn_kernel, tk=tk),
        grid=(N_KV_HEADS, S // tq),
        in_specs=[
            pl.BlockSpec((N_KV_HEADS * KN_ROWS, LANES), lambda g, i: (0, 0)),
            pl.BlockSpec((None, n_tiles, HEAD_DIM, tile_lanes), lambda g, i: (g, i, 0, 0)),
            pl.BlockSpec((None, S, HEAD_DIM), lambda g, i: (g, 0, 0)),
            pl.BlockSpec((None, HEAD_DIM, S), lambda g, i: (g, 0, 0)),
        ],
        out_specs=pl.BlockSpec((tq, GROUP * HEAD_DIM), lambda g, i: (i, g)),
        out_shape=jax.ShapeDtypeStruct((S, ATTN_WIDTH), BF16),
        scratch_shapes=[
            pltpu.VMEM((HEAD_DIM, lanes), BF16),
            pltpu.VMEM((SUBLANES, lanes), F32),
            pltpu.VMEM((SUBLANES, lanes), F32),
            pltpu.VMEM((HEAD_DIM // SUBLANES, SUBLANES, lanes), F32),
            pltpu.VMEM((2, tko // SUBLANES, SUBLANES, lanes), F32),
            pltpu.VMEM((2, SUBLANES, lanes), F32),
            pltpu.VMEM((2, tk, lanes), BF16),
            pltpu.VMEM((2, SUBLANES, lanes), F32),
        ],
        compiler_params=pltpu.CompilerParams(
            dimension_semantics=("arbitrary", "arbitrary"), vmem_limit_bytes=_vmem_limit(est)),
        name="attention",
    )(k_norm2, q_t, k, v_t)


LRU_T = 512
SCAN_UNROLL = 16
assert LRU_WIDTH == SUBLANES * LANES and LRU_BLOCKS == SUBLANES


def _lru_gates(prev_ref, cur_ref, next_ref, ext_tm, xc_tm, a_tm, b_tm, has_prev, has_next,
               cw_ref, cb_ref, w_ref, bias_ref, lam_ref):
    rows = cur_ref.shape[0]
    T = rows // LRU_BLOCKS
    halo = prev_ref.shape[0]
    ext_tm[0:halo, :] = jnp.where(has_prev, prev_ref[...], 0.0)
    ext_tm[halo:halo + rows, :] = cur_ref[...]
    ext_tm[halo + rows:2 * halo + rows, :] = jnp.where(has_next, next_ref[...], 0.0)
    base = halo - CONV_LEFT * LRU_BLOCKS
    xh = cb_ref[...]
    for j in range(CONV_W):
        start = base + j * LRU_BLOCKS
        tap = ext_tm[start:start + rows, :].reshape(T, LRU_BLOCKS, LRU_BW)
        xh = xh + cw_ref[j] * tap
    xc_tm[...] = xh.reshape(rows, LRU_BW)
    lam = lam_ref[...]
    neg = -lam
    softplus = jnp.maximum(neg, 0.0) + jnp.log1p(jnp.exp(-jnp.abs(neg)))
    half_rate = (-0.5 * LRU_C * LOG2_E) * softplus
    for n in range(LRU_BLOCKS):
        cols = slice(n * LRU_BW, (n + 1) * LRU_BW)
        block_n = pl.ds(n, T, stride=LRU_BLOCKS)
        xb = xc_tm[block_n, :]
        z = jnp.dot(xb.astype(BF16), w_ref[n], preferred_element_type=F32)
        tanh_r = jnp.tanh(z[:, :LRU_BW] + bias_ref[0:1, cols])
        tanh_i = jnp.tanh(z[:, LRU_BW:] + bias_ref[1:2, cols])
        rate = half_rate[:, cols]
        a = jnp.exp2(tanh_r * rate + rate)
        one_minus_a2 = 1.0 - a * a
        mult = one_minus_a2 * lax.rsqrt(jnp.maximum(one_minus_a2, TINY))
        half_in = mult * xb
        a_tm[block_n, :] = a
        b_tm[block_n, :] = half_in * tanh_i + half_in


def _lru_scan(af_tm, bf_tm, ab_tm, bb_tm, hf_ref, hb_ref, tiles_f, tiles_b, carry_ref):
    T = af_tm.shape[0] // LRU_BLOCKS
    U = SCAN_UNROLL
    tile_rows = LRU_BLOCKS * SUBLANES

    def step_rows(base_t, off):
        return pl.ds(pl.multiple_of(base_t * LRU_BLOCKS, U * LRU_BLOCKS) + off * LRU_BLOCKS,
                     LRU_BLOCKS)

    def scatter(tiles, base_t, off, h):
        start = (pl.multiple_of(base_t * LRU_BLOCKS, U * LRU_BLOCKS)
                 + (off // SUBLANES) * tile_rows + off % SUBLANES)
        tiles[pl.ds(start, LRU_BLOCKS, stride=SUBLANES), :] = h

    def pair(a_tm, b_tm, tiles, base_t, off0, off1, h):
        a0, b0 = a_tm[step_rows(base_t, off0), :], b_tm[step_rows(base_t, off0), :]
        a1, b1 = a_tm[step_rows(base_t, off1), :], b_tm[step_rows(base_t, off1), :]
        h0 = a0 * h + b0
        h1 = (a1 * a0) * h + (a1 * b0 + b1)
        scatter(tiles, base_t, off0, h0)
        scatter(tiles, base_t, off1, h1)
        return h1

    def store(out_ref, tiles, base_t):
        grp = tiles[pl.ds(pl.multiple_of(base_t * LRU_BLOCKS, U * LRU_BLOCKS), U * LRU_BLOCKS), :]
        grp = grp.reshape(U // SUBLANES, LRU_BLOCKS, SUBLANES, LRU_BW)
        rows = pl.ds(pl.multiple_of(base_t, U), U)
        for n in range(LRU_BLOCKS):
            out_ref[rows, n * LRU_BW:(n + 1) * LRU_BW] = (
                grp[:, n].reshape(U, LRU_BW).astype(out_ref.dtype))

    def body(i, carry):
        h_f, h_b = carry
        base_f = i * U
        base_b = T - U - i * U
        for u in range(0, U, 2):
            h_f = pair(af_tm, bf_tm, tiles_f, base_f, u, u + 1, h_f)
            h_b = pair(ab_tm, bb_tm, tiles_b, base_b, U - 1 - u, U - 2 - u, h_b)
        store(hf_ref, tiles_f, base_f)
        store(hb_ref, tiles_b, base_b)
        return h_f, h_b

    h_f, h_b = lax.fori_loop(0, T // U, body, (carry_ref[0], carry_ref[1]))
    carry_ref[0] = h_f
    carry_ref[1] = h_b


def _lru_kernel(fp_ref, fc_ref, fn_ref, bp_ref, bc_ref, bn_ref, cw_ref, cb_ref, w_ref, bias_ref,
                lam_ref, hf_ref, hb_ref, ext_tm, xc_tm, af_tm, bf_tm, ab_tm, bb_tm, tiles_f,
                tiles_b, carry_ref):
    c = pl.program_id(0)
    n_chunks = pl.num_programs(0)

    @pl.when(c == 0)
    def _():
        carry_ref[...] = jnp.zeros(carry_ref.shape, F32)

    _lru_gates(fp_ref, fc_ref, fn_ref, ext_tm, xc_tm, af_tm, bf_tm, c > 0, c < n_chunks - 1,
               cw_ref, cb_ref, w_ref.at[0], bias_ref.at[0], lam_ref.at[0])
    _lru_gates(bp_ref, bc_ref, bn_ref, ext_tm, xc_tm, ab_tm, bb_tm, c < n_chunks - 1, c > 0,
               cw_ref, cb_ref, w_ref.at[1], bias_ref.at[1], lam_ref.at[1])
    _lru_scan(af_tm, bf_tm, ab_tm, bb_tm, hf_ref, hb_ref, tiles_f, tiles_b, carry_ref)


LRU_HALO = SUBLANES


def _lru(xr_tm, conv_w_tm, conv_b_tm, w_gate, bias_gate, lam):
    S = xr_tm.shape[0] // LRU_BLOCKS
    T = LRU_T
    assert S % T == 0 and T % SCAN_UNROLL == 0 and T % LRU_HALO == 0
    assert CONV_LEFT <= LRU_HALO and CONV_W - 1 - CONV_LEFT <= LRU_HALO
    time_major = pltpu.VMEM((T * LRU_BLOCKS, LRU_BW), F32)
    n_chunks = S // T
    per = T // LRU_HALO
    last_blk = S // LRU_HALO - 1
    W = LRU_WIDTH
    chunk = (T * LRU_BLOCKS, LRU_BW)
    halo = (LRU_HALO * LRU_BLOCKS, LRU_BW)
    const2 = lambda c: (0, 0)
    est = (2 * 2 * (T * W * 4 + 2 * SUBLANES * W * 4) + 2 * 2 * T * W * 4
           + 2 * 2 * LRU_BLOCKS * LRU_BW * 2 * LRU_BW * 2 + (8 * T + 64) * W * 4 + 6 * T * W * 4)
    return pl.pallas_call(
        _lru_kernel,
        grid=(n_chunks,),
        in_specs=[
            pl.BlockSpec(halo, lambda c: (jnp.maximum(c * per - 1, 0), 0)),
            pl.BlockSpec(chunk, lambda c: (c, 0)),
            pl.BlockSpec(halo, lambda c: (jnp.minimum((c + 1) * per, last_blk), 0)),
            pl.BlockSpec(halo, lambda c: (jnp.maximum((n_chunks - 1 - c) * per - 1, 0), 0)),
            pl.BlockSpec(chunk, lambda c: (n_chunks - 1 - c, 0)),
            pl.BlockSpec(halo, lambda c: (jnp.minimum((n_chunks - c) * per, last_blk), 0)),
            pl.BlockSpec((CONV_W, LRU_BLOCKS, LRU_BW), lambda c: (0, 0, 0)),
            pl.BlockSpec((LRU_BLOCKS, LRU_BW), const2),
            pl.BlockSpec((2, LRU_BLOCKS, LRU_BW, 2 * LRU_BW), lambda c: (0, 0, 0, 0)),
            pl.BlockSpec((2, 2, W), lambda c: (0, 0, 0)),
            pl.BlockSpec((2, 1, W), lambda c: (0, 0, 0)),
        ],
        out_specs=[
            pl.BlockSpec((T, W), lambda c: (c, 0)),
            pl.BlockSpec((T, W), lambda c: (n_chunks - 1 - c, 0)),
        ],
        out_shape=[jax.ShapeDtypeStruct((S, W), BF16), jax.ShapeDtypeStruct((S, W), BF16)],
        scratch_shapes=[
            pltpu.VMEM(((T + 2 * LRU_HALO) * LRU_BLOCKS, LRU_BW), F32),
            time_major,
            time_major, time_major,
            time_major, time_major,
            time_major, time_major,
            pltpu.VMEM((2, SUBLANES, LRU_BW), F32),
        ],
        compiler_params=pltpu.CompilerParams(
            dimension_semantics=("arbitrary",), vmem_limit_bytes=_vmem_limit(est)),
        name="rg_lru",
    )(xr_tm, xr_tm, xr_tm, xr_tm, xr_tm, xr_tm, conv_w_tm, conv_b_tm, w_gate, bias_gate, lam)


OUT_TM = 512


def _out_proj_kernel(ao_ref, ga_ref, hf_ref, hb_ref, gl_ref, x_ref, anw_ref, lnw_ref, w_ref,
                     o_ref):
    def inv_rms(v):
        return lax.rsqrt(jnp.mean(v * v, axis=-1, keepdims=True) + EPS)

    ao = ao_ref[...].astype(F32)
    hl = hf_ref[...].astype(F32) + hb_ref[...].astype(F32)
    ya = (ao * anw_ref[...] * ga_ref[...].astype(F32)).astype(BF16)
    yl = (hl * lnw_ref[...] * gl_ref[...].astype(F32)).astype(BF16)
    mix_a = jnp.dot(ya, w_ref[0:ATTN_WIDTH, :], preferred_element_type=F32) * inv_rms(ao)
    mix_l = jnp.dot(yl, w_ref[ATTN_WIDTH:MIX_WIDTH, :], preferred_element_type=F32) * inv_rms(hl)
    o_ref[...] = x_ref[...] + mix_a + mix_l


def _out_proj(attn_out, g_attn, h_fwd, h_bwd, g_lru, x2, attn_norm_w, lru_norm_w, w_out_bf16):
    S = x2.shape[0]
    tm = OUT_TM
    row = lambda i: (i, 0)
    const = lambda i: (0, 0)
    est = (2 * 5 * tm * ATTN_WIDTH * 2 + 2 * 2 * tm * D_MODEL * 4 + MIX_WIDTH * D_MODEL * 2
           + 4 * tm * D_MODEL * 4)
    return pl.pallas_call(
        _out_proj_kernel,
        grid=(S // tm,),
        in_specs=[
            pl.BlockSpec((tm, ATTN_WIDTH), row),
            pl.BlockSpec((tm, ATTN_WIDTH), row),
            pl.BlockSpec((tm, LRU_WIDTH), row),
            pl.BlockSpec((tm, LRU_WIDTH), row),
            pl.BlockSpec((tm, LRU_WIDTH), row),
            pl.BlockSpec((tm, D_MODEL), row),
            pl.BlockSpec((1, ATTN_WIDTH), const),
            pl.BlockSpec((1, LRU_WIDTH), const),
            pl.BlockSpec((MIX_WIDTH, D_MODEL), const, pipeline_mode=pl.Buffered(1)),
        ],
        out_specs=pl.BlockSpec((tm, D_MODEL), row),
        out_shape=jax.ShapeDtypeStruct((S, D_MODEL), F32),
        compiler_params=pltpu.CompilerParams(
            dimension_semantics=("arbitrary",), vmem_limit_bytes=_vmem_limit(est)),
        name="out_proj",
    )(attn_out, g_attn, h_fwd, h_bwd, g_lru, x2, attn_norm_w, lru_norm_w, w_out_bf16)


def _rope_tables(seq_len, tile_rows):
    rows = seq_len // GRID_W
    inv_freq = ROPE_THETA ** (-jnp.arange(ROPE_PAIRS, dtype=F32) / ROPE_PAIRS)
    ang_r = jnp.arange(rows, dtype=F32)[:, None] * inv_freq[None, :]
    ang_c = jnp.arange(GRID_W, dtype=F32)[:, None] * inv_freq[None, :]
    cr, sr, cc, sc = jnp.cos(ang_r), jnp.sin(ang_r), jnp.cos(ang_c), jnp.sin(ang_c)
    zr, zc = jnp.zeros_like(cr), jnp.zeros_like(cc)
    reps = tile_rows // GRID_W
    return (jnp.concatenate([cr, cr, zr, zr], axis=-1),
            jnp.concatenate([-sr, sr, zr, zr], axis=-1),
            jnp.tile(jnp.concatenate([zc, zc, cc, cc], axis=-1), (reps, 1)),
            jnp.tile(jnp.concatenate([zc, zc, -sc, sc], axis=-1), (reps, 1)))


def _layer(x2, norm_w, w_in, q_norm_w, k_norm_w, conv_w, conv_b, lru_wa, lru_ba, lru_wx, lru_bx,
           lru_lambda, attn_norm_w, lru_norm_w, w_out):
    S = x2.shape[0]
    q_t, k, v_t, k_norm2, g_attn, xr, g_lru = _in_proj(
        x2, norm_w.reshape(1, D_MODEL), w_in.astype(BF16), q_norm_w.reshape(1, HEAD_DIM),
        k_norm_w.reshape(1, HEAD_DIM), *_rope_tables(S, PROJ_TM))
    attn_out = _attention(k_norm2, q_t, k, v_t)
    w_gate = jnp.concatenate([lru_wa, lru_wx], axis=-1).astype(BF16)
    bias_gate = 0.5 * jnp.stack([lru_ba, lru_bx], axis=1)
    h_fwd, h_bwd = _lru(xr, 0.5 * conv_w.reshape(CONV_W, LRU_BLOCKS, LRU_BW),
                        0.5 * conv_b.reshape(LRU_BLOCKS, LRU_BW),
                        w_gate, bias_gate, lru_lambda.reshape(2, 1, LRU_WIDTH))
    return _out_proj(attn_out, g_attn, h_fwd, h_bwd, g_lru, x2,
                     attn_norm_w.reshape(1, ATTN_WIDTH), lru_norm_w.reshape(1, LRU_WIDTH),
                     w_out.astype(BF16))


def kernel(x, norm_w, w_in, q_norm_w, k_norm_w, conv_w, conv_b, lru_wa, lru_ba, lru_wx, lru_bx,
           lru_lambda, attn_norm_w, lru_norm_w, w_out):
    B, S, D = x.shape
    assert D == D_MODEL and S % GRID_W == 0
    outs = [_layer(x[b], norm_w, w_in, q_norm_w, k_norm_w, conv_w, conv_b, lru_wa, lru_ba,
                   lru_wx, lru_bx, lru_lambda, attn_norm_w, lru_norm_w, w_out)
            for b in range(B)]
    return jnp.stack(outs, axis=0) if B > 1 else outs[0][None]
```
